```python
import jax, jax.numpy as jnp
from jax import lax
import numpy as np

D_MODEL = 2048
BATCH = 2
SEQ = 16384
DEPTH = 4

HEAD_DIM = 64
A_HEADS = 16
A_WIDTH = A_HEADS * HEAD_DIM
A_LORA_W = 64
A_LORA_A = 64
A_LORA_G = 128
A_COLS = 3 * A_WIDTH + A_LORA_W + A_LORA_A + A_LORA_G
A_GN_EPS = 64e-5
B_HEADS = 8
B_DK = 128
B_DV = 128
B_WIDTH = B_HEADS * B_DV
B_COLS = 2 * B_HEADS * B_DK + 2 * B_WIDTH
B_CHUNK = 32
EVEN_COLS = A_COLS + B_COLS
C_HEADS = 16
C_WIDTH = C_HEADS * HEAD_DIM
C_PAIRS = ((128, 1), (512, 4), (2048, 16))
D_HEADS = 4
D_DQK = 128
D_DV = 256
D_QK_WIDTH = D_HEADS * D_DQK
D_WIDTH = D_HEADS * D_DV
D_CONV = 4
D_CHUNK = 64
ODD_COLS = 3 * C_WIDTH + 2 * D_QK_WIDTH + 2 * D_WIDTH + 2 * D_HEADS
MIX_WIDTH = A_WIDTH + B_WIDTH
N_GROUPS = 4
EXPERTS_PER_GROUP = 8
N_EXPERTS = N_GROUPS * EXPERTS_PER_GROUP
TOP_K = 2
D_EXPERT = 512
MOE_BLOCK = 128
N_EVEN = (DEPTH + 1) // 2
N_ODD = DEPTH // 2
DN_ALPHA = (2 * DEPTH) ** 0.25
DN_BETA = (8 * DEPTH) ** -0.25
LN_EPS = 1e-5
RMS_EPS = 1e-6
F32 = jnp.float32

kernel_name = 'hybrid_rwkv7_hgrn2_dilattn_mlstm_hmoe_deepnorm'


def layer_norm(x, w, b):
    xf = x.astype(F32)
    mu = jnp.mean(xf, -1, keepdims=True)
    var = jnp.mean(jnp.square(xf - mu), -1, keepdims=True)
    return (xf - mu) * lax.rsqrt(var + LN_EPS) * w.astype(F32) + b.astype(F32)


def token_shift(t):
    return jnp.pad(t, ((0, 0), (1, 0), (0, 0)))[:, :-1]


def causal_depthwise_conv(t, w, b):
    ch = t.shape[-1]
    out = lax.conv_general_dilated(t, w[:, None, :].astype(t.dtype), window_strides=(1,),
                                   padding=[(w.shape[0] - 1, 0)],
                                   dimension_numbers=('NWC', 'WIO', 'NWC'),
                                   feature_group_count=ch)
    return out + b


def to_chunks(t, c):
    b_, s, h, f = t.shape
    return t.reshape(b_, s // c, c, h, f).transpose(1, 0, 3, 2, 4)


def from_chunks(t):
    n, b_, h, c, f = t.shape
    return t.transpose(1, 0, 3, 2, 4).reshape(b_, n * c, h, f)


def rwkv7_scan(r, w, k, v, a_vec, b_vec):
    b_, s, h, n = r.shape
    def step(state, inp):
        r_t, w_t, k_t, v_t, a_t, bb_t = inp
        sa = jnp.einsum('bhvk,bhk->bhv', state, a_t)
        state = (state * w_t[:, :, None, :] + sa[..., None] * bb_t[:, :, None, :]
                 + v_t[..., None] * k_t[:, :, None, :])
        return state, jnp.einsum('bhvk,bhk->bhv', state, r_t)
    xs = tuple(jnp.moveaxis(t.astype(F32), 1, 0) for t in (r, w, k, v, a_vec, b_vec))
    _, ys = lax.scan(step, jnp.zeros((b_, h, n, n), F32), xs)
    return jnp.moveaxis(ys, 0, 1)


def rwkv7_mix(pa, mu, w0, w2, a0, a2, g2, k_k, k_a, r_k, lnx_w, lnx_b):
    b_, s, _ = pa.shape
    pa = (pa + (token_shift(pa) - pa) * mu).astype(F32)
    r, k, v, xw, xa, xg = jnp.split(
        pa, [A_WIDTH, 2 * A_WIDTH, 3 * A_WIDTH, 3 * A_WIDTH + A_LORA_W,
             3 * A_WIDTH + A_LORA_W + A_LORA_A], axis=-1)
    w_log = -jax.nn.softplus(-(w0 + jnp.tanh(xw) @ w2)) - 0.5
    decay = jnp.exp(-jnp.exp(w_log))
    a = jax.nn.sigmoid(a0 + xa @ a2)
    g = jax.nn.sigmoid(xg) @ g2
    heads = lambda t: t.reshape(b_, s, A_HEADS, HEAD_DIM)
    kk = heads(k * k_k)
    kk = kk / jnp.maximum(jnp.linalg.norm(kk, axis=-1, keepdims=True), 1e-12)
    k = k * (1.0 + (a - 1.0) * k_a)
    r, k, v, a, decay = heads(r), heads(k), heads(v), heads(a), heads(decay)
    y = rwkv7_scan(r, decay, k, v, -kk, kk * a)
    mu_y = jnp.mean(y, -1, keepdims=True)
    var_y = jnp.mean(jnp.square(y - mu_y), -1, keepdims=True)
    y = ((y - mu_y) * lax.rsqrt(var_y + A_GN_EPS)).reshape(b_, s, A_WIDTH) * lnx_w + lnx_b
    y = y + (jnp.sum(r * k * r_k, -1, keepdims=True) * v).reshape(b_, s, A_WIDTH)
    return y * g


def gla_chunk_scan(q, k, v, log_f, chunk):
    b_, s, h, dk = q.shape
    dv = v.shape[-1]
    causal = jnp.tril(jnp.ones((chunk, chunk), bool))[:, :, None]
    def step(state, inp):
        qc, kc, vc, gc = inp
        bcum = jnp.cumsum(gc, axis=2)
        o_inter = jnp.einsum('bhtk,bhkv->bhtv', qc * jnp.exp(bcum), state)
        rel = jnp.exp(jnp.where(causal, bcum[:, :, :, None, :] - bcum[:, :, None, :, :], -jnp.inf))
        attn = jnp.einsum('bhtsk,bhsk->bhts', qc[:, :, :, None, :] * rel, kc)
        out = o_inter + jnp.einsum('bhts,bhsv->bhtv', attn, vc)
        b_last = bcum[:, :, -1:, :]
        state = (jnp.exp(b_last[:, :, 0, :])[..., None] * state
                 + jnp.einsum('bhsk,bhsv->bhkv', kc * jnp.exp(b_last - bcum), vc))
        return state, out
    xs = (to_chunks(q, chunk), to_chunks(k, chunk), to_chunks(v, chunk), to_chunks(log_f, chunk))
    _, out = lax.scan(step, jnp.zeros((b_, h, dk, dv), F32), xs)
    return from_chunks(out)


def hgrn2_mix(pb, lb, norm_w):
    b_, s, _ = pb.shape
    pb = pb.astype(F32)
    q, f, i, g = jnp.split(pb, [B_HEADS * B_DK, 2 * B_HEADS * B_DK, 2 * B_HEADS * B_DK + B_WIDTH], -1)
    q = jax.nn.silu(q)
    log_f = jnp.logaddexp(jnp.log(lb), jnp.log1p(-lb) + jax.nn.log_sigmoid(f))
    k = (1.0 - lb) * jax.nn.sigmoid(-f)
    hk = lambda t: t.reshape(b_, s, B_HEADS, B_DK)
    o = gla_chunk_scan(hk(q), hk(k), i.reshape(b_, s, B_HEADS, B_DV), hk(log_f), B_CHUNK)
    o = o * lax.rsqrt(jnp.mean(o * o, -1, keepdims=True) + RMS_EPS) * norm_w
    return o.reshape(b_, s, B_WIDTH) * jax.nn.silu(g)


def dilated_attention(q, k, v):
    b_, s, h, e = q.shape
    scale = e ** -0.5
    o_list, m_list, l_list = [], [], []
    for window, dil in C_PAIRS:
        nk = window // dil
        length = s // dil
        nb = -(-length // nk)
        lp = nb * nk
        def split_stride(t):
            t = t.astype(F32).reshape(b_, length, dil, h, e).transpose(0, 2, 3, 1, 4)
            return jnp.pad(t, ((0, 0), (0, 0), (0, 0), (0, lp - length), (0, 0)))
        qs, ks, vs = split_stride(q), split_stride(k), split_stride(v)
        blocks = lambda t: t.reshape(b_, dil, h, nb, nk, e)
        def banded(t):
            prev = jnp.pad(t, ((0, 0), (0, 0), (0, 0), (nk, 0), (0, 0)))[:, :, :, :lp]
            return jnp.concatenate([blocks(prev), blocks(t)], axis=4)
        sc = jnp.einsum('bdhnqe,bdhnke->bdhnqk', blocks(qs), banded(ks)) * scale
        qi = jnp.arange(nk)[:, None]
        kj = jnp.arange(2 * nk)[None, :]
        dist = qi + nk - kj
        kpos = jnp.arange(nb)[:, None, None] * nk - nk + kj
        valid = (dist >= 0) & (dist <= nk) & (kpos >= 0)
        sc = jnp.where(valid, sc, -jnp.inf)
        m = jnp.max(sc, -1)
        p = jnp.exp(sc - m[..., None])
        o = jnp.einsum('bdhnqk,bdhnke->bdhnqe', p, banded(vs))
        def merge_stride(t):
            fdim = t.shape[-1]
            t = t.reshape(b_, dil, h, lp, fdim)[:, :, :, :length]
            return t.transpose(0, 3, 1, 2, 4).reshape(b_, s, h, fdim)
        o_list.append(merge_stride(o))
        m_list.append(merge_stride(m[..., None])[..., 0])
        l_list.append(merge_stride(jnp.sum(p, -1)[..., None])[..., 0])
    m_all = jnp.stack(m_list)
    l_all = jnp.stack(l_list)
    o_all = jnp.stack(o_list)
    wgt = jnp.exp(m_all - jnp.max(m_all, 0))
    return jnp.sum(wgt[..., None] * o_all, 0) / jnp.sum(wgt * l_all, 0)[..., None]


def mlstm_chunk_scan(q, k, v, log_i, log_f, chunk):
    b_, s, h, dk = q.shape
    dv = v.shape[-1]
    n = s // chunk
    gate_chunks = lambda t: t.reshape(b_, n, chunk, h).transpose(1, 0, 3, 2)
    causal = jnp.tril(jnp.ones((chunk, chunk), bool))
    def step(carry, inp):
        c_mat, n_vec, m = carry
        qc, kc, vc, ic, fc = inp
        fcum = jnp.cumsum(fc, -1)
        log_d = jnp.where(causal, fcum[..., :, None] - fcum[..., None, :] + ic[..., None, :], -jnp.inf)
        log_inter = fcum + m[..., None]
        m_t = jnp.maximum(jnp.max(log_d, -1), log_inter)
        w_inter = jnp.exp(log_inter - m_t)
        scores = jnp.einsum('bhtk,bhsk->bhts', qc, kc) * jnp.exp(log_d - m_t[..., None])
        num = (w_inter[..., None] * jnp.einsum('bhtk,bhkv->bhtv', qc, c_mat)
               + jnp.einsum('bhts,bhsv->bhtv', scores, vc))
        den = w_inter * jnp.einsum('bhtk,bhk->bht', qc, n_vec) + jnp.sum(scores, -1)
        out = num / jnp.maximum(jnp.abs(den), jnp.exp(-m_t))[..., None]
        f_tot = fcum[..., -1]
        log_w = f_tot[..., None] - fcum + ic
        m_new = jnp.maximum(f_tot + m, jnp.max(log_w, -1))
        w_s = jnp.exp(log_w - m_new[..., None])
        w_old = jnp.exp(f_tot + m - m_new)
        c_mat = w_old[..., None, None] * c_mat + jnp.einsum('bhs,bhsk,bhsv->bhkv', w_s, kc, vc)
        n_vec = w_old[..., None] * n_vec + jnp.einsum('bhs,bhsk->bhk', w_s, kc)
        return (c_mat, n_vec, m_new), out
    init = (jnp.zeros((b_, h, dk, dv), F32), jnp.zeros((b_, h, dk), F32), jnp.zeros((b_, h), F32))
    xs = (to_chunks(q, chunk), to_chunks(k, chunk), to_chunks(v, chunk),
          gate_chunks(log_i), gate_chunks(log_f))
    _, out = lax.scan(step, init, xs)
    return from_chunks(out)


def mlstm_mix(pd, conv_w, conv_b, b_i, b_f):
    b_, s, _ = pd.shape
    o1 = 2 * D_QK_WIDTH
    qk, v, o, gi, gf = jnp.split(pd, [o1, o1 + D_WIDTH, o1 + 2 * D_WIDTH, o1 + 2 * D_WIDTH + D_HEADS], -1)
    qk = jax.nn.silu(causal_depthwise_conv(qk, conv_w, conv_b)).astype(F32)
    q, k = jnp.split(qk, 2, -1)
    hq = lambda t: t.reshape(b_, s, D_HEADS, D_DQK)
    log_i = (gi + b_i).astype(F32)
    log_f = jax.nn.log_sigmoid((gf + b_f).astype(F32))
    h = mlstm_chunk_scan(hq(q), hq(k) * D_DQK ** -0.5, v.astype(F32).reshape(b_, s, D_HEADS, D_DV),
                         log_i, log_f, D_CHUNK)
    return jax.nn.sigmoid(o.astype(F32)) * h.reshape(b_, s, D_WIDTH)


def grouped_expert_ffn(xf, expert_id, token_id, weight, w1, w3, w2):
    t, d = xf.shape
    n = expert_id.shape[0]
    e = w1.shape[0]
    blk = MOE_BLOCK
    rows = -(-(n + e * (blk - 1)) // blk) * blk
    order = jnp.argsort(expert_id)
    e_sorted = expert_id[order]
    counts = jnp.bincount(expert_id, length=e)
    padded = -(-counts // blk) * blk
    start = jnp.cumsum(counts) - counts
    pstart = jnp.cumsum(padded) - padded
    dest = pstart[e_sorted] + jnp.arange(n) - start[e_sorted]
    row_tok = jnp.full((rows,), t, jnp.int32).at[dest].set(token_id[order].astype(jnp.int32))
    row_w = jnp.zeros((rows,), F32).at[dest].set(weight[order].astype(F32))
    n_blocks = rows // blk
    block_exp = jnp.clip(jnp.searchsorted(jnp.cumsum(padded), jnp.arange(n_blocks) * blk, side='right'), 0, e - 1)
    x_pad = jnp.concatenate([xf, jnp.zeros((1, d), xf.dtype)], 0)
    xs = x_pad[row_tok].reshape(n_blocks, blk, d)
    def run(args):
        xb, eb = args
        hb = jax.nn.silu(xb @ w1[eb]) * (xb @ w3[eb])
        return hb @ w2[eb]
    ys = lax.map(run, (xs, block_exp)).reshape(rows, d)
    out = jnp.zeros((t + 1, d), F32).at[row_tok].add(ys.astype(F32) * row_w[:, None])
    return out[:t]


def hier_moe(x, wg, bg, we, be, w1, w3, w2):
    b_, s, d = x.shape
    t = b_ * s
    xf = x.reshape(t, d)
    g_logits = (xf @ wg).astype(F32) + bg
    g_prob = jax.nn.softmax(g_logits, -1)
    g_top = jnp.argmax(g_logits, -1)
    p_group = jnp.take_along_axis(g_prob, g_top[:, None], -1)[:, 0]
    e_all = jnp.einsum('td,gde->tge', xf, we).astype(F32) + be
    e_logits = jnp.take_along_axis(e_all, g_top[:, None, None], axis=1)[:, 0]
    top_v, top_i = lax.top_k(e_logits, TOP_K)
    gate = p_group[:, None] * jax.nn.softmax(top_v, -1)
    expert = g_top[:, None] * EXPERTS_PER_GROUP + top_i
    token = jnp.repeat(jnp.arange(t), TOP_K)
    y = grouped_expert_ffn(xf, expert.reshape(-1), token, gate.reshape(-1), w1, w3, w2)
    return y.reshape(b_, s, d).astype(x.dtype)


def setup_inputs(seed: int = 0) -> dict:
    key = jax.random.key(seed)
    ks = iter(jax.random.split(key, 40))
    nrm = lambda shape, scale: jax.random.normal(next(ks), shape, F32) * scale
    uni = lambda shape, lo, hi: jax.random.uniform(next(ks), shape, F32, lo, hi)
    return {
        'x': nrm((BATCH, SEQ, D_MODEL), 1.0),
        'ev_w_in': nrm((N_EVEN, D_MODEL, EVEN_COLS), D_MODEL ** -0.5),
        'ev_w_out': nrm((N_EVEN, MIX_WIDTH, D_MODEL), MIX_WIDTH ** -0.5 * DN_BETA),
        'rwkv_mu': uni((N_EVEN, A_COLS), 0.0, 1.0),
        'rwkv_w0': uni((N_EVEN, A_WIDTH), -6.0, -1.0),
        'rwkv_w2': nrm((N_EVEN, A_LORA_W, A_WIDTH), 0.5 * A_LORA_W ** -0.5),
        'rwkv_a0': nrm((N_EVEN, A_WIDTH), 0.1),
        'rwkv_a2': nrm((N_EVEN, A_LORA_A, A_WIDTH), 0.5 * A_LORA_A ** -0.5),
        'rwkv_g2': nrm((N_EVEN, A_LORA_G, A_WIDTH), A_LORA_G ** -0.5),
        'rwkv_kk': 0.85 + nrm((N_EVEN, A_WIDTH), 0.05),
        'rwkv_ka': 1.0 + nrm((N_EVEN, A_WIDTH), 0.05),
        'rwkv_rk': nrm((N_EVEN, A_HEADS, HEAD_DIM), 0.1),
        'rwkv_lnx_w': 1.0 + nrm((N_EVEN, A_WIDTH), 0.02),
        'rwkv_lnx_b': nrm((N_EVEN, A_WIDTH), 0.02),
        'hgrn_lb': nrm((N_EVEN, B_HEADS * B_DK), 0.5),
        'hgrn_norm_w': 1.0 + nrm((N_EVEN, B_DV), 0.02),
        'od_w_in': nrm((N_ODD, D_MODEL, ODD_COLS), D_MODEL ** -0.5),
        'od_w_out': nrm((N_ODD, MIX_WIDTH, D_MODEL), MIX_WIDTH ** -0.5 * DN_BETA),
        'mlstm_conv_w': nrm((N_ODD, D_CONV, 2 * D_QK_WIDTH), D_CONV ** -0.5),
        'mlstm_conv_b': nrm((N_ODD, 2 * D_QK_WIDTH), 0.02),
        'mlstm_b_i': nrm((N_ODD, D_HEADS), 0.1),
        'mlstm_b_f': jnp.linspace(3.0, 6.0, D_HEADS, dtype=F32) + nrm((N_ODD, D_HEADS), 0.1),
        'ln_w': 1.0 + nrm((DEPTH, 2, D_MODEL), 0.02),
        'ln_b': nrm((DEPTH, 2, D_MODEL), 0.02),
        'moe_wg': nrm((DEPTH, D_MODEL, N_GROUPS), D_MODEL ** -0.5),
        'moe_bg': nrm((DEPTH, N_GROUPS), 0.01),
        'moe_we': nrm((DEPTH, N_GROUPS, D_MODEL, EXPERTS_PER_GROUP), D_MODEL ** -0.5),
        'moe_be': nrm((DEPTH, N_GROUPS, EXPERTS_PER_GROUP), 0.01),
        'moe_w1': nrm((DEPTH, N_EXPERTS, D_MODEL, D_EXPERT), D_MODEL ** -0.5),
        'moe_w3': nrm((DEPTH, N_EXPERTS, D_MODEL, D_EXPERT), D_MODEL ** -0.5),
        'moe_w2': nrm((DEPTH, N_EXPERTS, D_EXPERT, D_MODEL), D_EXPERT ** -0.5 * DN_BETA),
    }


def reference(x, ev_w_in, ev_w_out, rwkv_mu, rwkv_w0, rwkv_w2, rwkv_a0, rwkv_a2, rwkv_g2,
              rwkv_kk, rwkv_ka, rwkv_rk, rwkv_lnx_w, rwkv_lnx_b, hgrn_lb, hgrn_norm_w,
              od_w_in, od_w_out, mlstm_conv_w, mlstm_conv_b, mlstm_b_i, mlstm_b_f,
              ln_w, ln_b, moe_wg, moe_bg, moe_we, moe_be, moe_w1, moe_w3, moe_w2):
    lb_all = jnp.cumsum(jax.nn.softmax(hgrn_lb.astype(F32), axis=0), axis=0)
    lb_all = lb_all - lb_all[:1]
    for layer in range(DEPTH):
        j = layer // 2
        if layer % 2 == 0:
            p = x @ ev_w_in[j]
            ya = rwkv7_mix(p[..., :A_COLS], rwkv_mu[j], rwkv_w0[j], rwkv_w2[j], rwkv_a0[j],
                           rwkv_a2[j], rwkv_g2[j], rwkv_kk[j], rwkv_ka[j], rwkv_rk[j],
                           rwkv_lnx_w[j], rwkv_lnx_b[j])
            yb = hgrn2_mix(p[..., A_COLS:], lb_all[j], hgrn_norm_w[j])
            mix = jnp.concatenate([ya, yb], -1).astype(x.dtype) @ ev_w_out[j]
        else:
            b_, s, _ = x.shape
            p = x @ od_w_in[j]
            qc, kc, vc, pd = jnp.split(p, [C_WIDTH, 2 * C_WIDTH, 3 * C_WIDTH], -1)
            hc = lambda t: t.reshape(b_, s, C_HEADS, HEAD_DIM)
            yc = dilated_attention(hc(qc), hc(kc), hc(vc)).reshape(b_, s, C_WIDTH)
            yd = mlstm_mix(pd, mlstm_conv_w[j], mlstm_conv_b[j], mlstm_b_i[j], mlstm_b_f[j])
            mix = jnp.concatenate([yc, yd], -1).astype(x.dtype) @ od_w_out[j]
        x = layer_norm(DN_ALPHA * x + mix, ln_w[layer, 0], ln_b[layer, 0]).astype(x.dtype)
        ffn = hier_moe(x, moe_wg[layer], moe_bg[layer], moe_we[layer], moe_be[layer],
                       moe_w1[layer], moe_w3[layer], moe_w2[layer])
        x = layer_norm(DN_ALPHA * x + ffn, ln_w[layer, 1], ln_b[layer, 1]).astype(x.dtype)
    return x
```

```python
import functools

import jax
import jax.numpy as jnp
from jax import lax
from jax.experimental import pallas as pl
from jax.experimental.pallas import tpu as pltpu

F32 = jnp.float32
BF16 = jnp.bfloat16

HEAD_DIM = 64
A_HEADS = 16
A_WIDTH = A_HEADS * HEAD_DIM
A_LORA_W = 64
A_LORA_A = 64
A_LORA_G = 128
A_COLS = 3 * A_WIDTH + A_LORA_W + A_LORA_A + A_LORA_G
A_GN_EPS = 64e-5
B_HEADS = 8
B_DK = 128
B_DV = 128
B_WIDTH = B_HEADS * B_DV
C_HEADS = 16
C_WIDTH = C_HEADS * HEAD_DIM
C_PAIRS = ((128, 1), (512, 4), (2048, 16))
D_HEADS = 4
D_DQK = 128
D_DV = 256
D_QK_WIDTH = D_HEADS * D_DQK
D_WIDTH = D_HEADS * D_DV
D_CONV = 4
ODD_COLS = 3 * C_WIDTH + 2 * D_QK_WIDTH + 2 * D_WIDTH + 2 * D_HEADS
N_GROUPS = 4
EXPERTS_PER_GROUP = 8
N_EXPERTS = N_GROUPS * EXPERTS_PER_GROUP
LN_EPS = 1e-5
RMS_EPS = 1e-6

LANES = 128
VMEM_LIMIT = 56 * 1024 * 1024
RW_CHUNK = 64
GLA_CHUNK = 16
ML_CHUNK = 64
SEQ_BLOCK = 256
ATT_BLOCK = 128
MOE_BM = 256
TOK_BITS = 15

NN = (((1,), (0,)), ((), ()))
NT = (((1,), (1,)), ((), ()))


def _dot(a, b, dims=NN):
    return lax.dot_general(a.astype(BF16), b.astype(BF16), dims, preferred_element_type=F32)


def _split3(x):
    hi = x.astype(BF16)
    r1 = x - hi.astype(F32)
    mid = r1.astype(BF16)
    lo = (r1 - mid.astype(F32)).astype(BF16)
    return hi, mid, lo


def _dot_exact_lhs(mask, x, dims=NN):
    hi, mid, lo = _split3(x)
    m = mask.astype(BF16)
    d = lambda p: lax.dot_general(m, p, dims, preferred_element_type=F32)
    return d(hi) + d(mid) + d(lo)


def _dot_exact_rhs(x, mask):
    hi, mid, lo = _split3(x)
    m = mask.astype(BF16)
    d = lambda p: lax.dot_general(p, m, NN, preferred_element_type=F32)
    return d(hi) + d(mid) + d(lo)


def _sigmoid(x):
    return 1.0 / (1.0 + jnp.exp(-x))


def _silu(x):
    return x * _sigmoid(x)


def _log_sigmoid(x):
    return jnp.minimum(x, 0.0) - jnp.log1p(jnp.exp(-jnp.abs(x)))


def _params(*sem):
    return pltpu.CompilerParams(dimension_semantics=sem, vmem_limit_bytes=VMEM_LIMIT)


def _mm_kernel(x_ref, w_ref, o_ref):
    o_ref[...] = _dot(x_ref[...], w_ref[...])


def _matmul(x, w, tm=1024, tn=256):
    t, k = x.shape
    n = w.shape[1]
    tm = min(tm, t)
    return pl.pallas_call(
        _mm_kernel,
        name="in_proj",
        grid=(t // tm, n // tn),
        in_specs=[pl.BlockSpec((tm, k), lambda i, j: (i, 0)),
                  pl.BlockSpec((k, tn), lambda i, j: (0, j))],
        out_specs=pl.BlockSpec((tm, tn), lambda i, j: (i, j)),
        out_shape=jax.ShapeDtypeStruct((t, n), F32),
        compiler_params=_params("parallel", "arbitrary"),
    )(x, w)


def _layer_norm(y, w, b):
    mu = jnp.mean(y, -1, keepdims=True)
    yc = y - mu
    var = jnp.mean(yc * yc, -1, keepdims=True)
    return yc * lax.rsqrt(var + LN_EPS) * w + b


def _proj_ln_kernel(ya_ref, yb_ref, wa_ref, wb_ref, x_ref, g_ref, b_ref, o_ref, *, alpha):
    y = _dot(ya_ref[...], wa_ref[...]) + _dot(yb_ref[...], wb_ref[...]) + alpha * x_ref[...]
    o_ref[...] = _layer_norm(y, g_ref[...], b_ref[...])


def _proj_ln(ya, yb, w_out, x, g, b, alpha, tm=256):
    t, d = x.shape
    ka = ya.shape[1]
    kb = yb.shape[1]
    tm = min(tm, t)
    wa = w_out[:ka].astype(BF16)
    wb = w_out[ka:].astype(BF16)
    row = lambda i: (i, 0)
    fix = lambda i: (0, 0)
    return pl.pallas_call(
        functools.partial(_proj_ln_kernel, alpha=alpha),
        name="proj_ln",
        grid=(t // tm,),
        in_specs=[pl.BlockSpec((tm, ka), row), pl.BlockSpec((tm, kb), row),
                  pl.BlockSpec((ka, d), fix), pl.BlockSpec((kb, d), fix),
                  pl.BlockSpec((tm, d), row), pl.BlockSpec((1, d), fix), pl.BlockSpec((1, d), fix)],
        out_specs=pl.BlockSpec((tm, d), row),
        out_shape=jax.ShapeDtypeStruct((t, d), F32),
        compiler_params=_params("parallel"),
    )(ya, yb, wa, wb, x, g.reshape(1, d), b.reshape(1, d))


def _router_kernel(x_ref, wh_ref, wm_ref, b_ref, o_ref):
    xh, xm, _ = _split3(x_ref[...])
    wh = wh_ref[...]
    d = lambda a, w: lax.dot_general(a, w, NN, preferred_element_type=F32)
    logits = d(xh, wh) + d(xh, wm_ref[...]) + d(xm, wh) + b_ref[...]
    tm = logits.shape[0]
    lane = lax.broadcasted_iota(jnp.int32, (tm, LANES), 1)
    neg = -jnp.inf
    gl = jnp.where(lane < N_GROUPS, logits, neg)
    gmax = jnp.max(gl, -1, keepdims=True)
    g_top = jnp.min(jnp.where(gl == gmax, lane, LANES), -1, keepdims=True)
    p_group = 1.0 / jnp.sum(jnp.exp(gl - gmax), -1, keepdims=True)
    e_lane = lane - N_GROUPS
    in_group = (e_lane >= 0) & (e_lane < N_EXPERTS) & ((e_lane // EXPERTS_PER_GROUP) == g_top)
    el = jnp.where(in_group, logits, neg)
    v1 = jnp.max(el, -1, keepdims=True)
    i1 = jnp.min(jnp.where(el == v1, lane, LANES), -1, keepdims=True)
    el2 = jnp.where(lane == i1, neg, el)
    v2 = jnp.max(el2, -1, keepdims=True)
    i2 = jnp.min(jnp.where(el2 == v2, lane, LANES), -1, keepdims=True)
    e2 = jnp.exp(v2 - v1)
    w1 = 1.0 / (1.0 + e2)
    w2 = e2 / (1.0 + e2)
    out = jnp.where(lane == 0, (i1 - N_GROUPS).astype(F32),
                    jnp.where(lane == 1, (i2 - N_GROUPS).astype(F32),
                              jnp.where(lane == 2, p_group * w1,
                                        jnp.where(lane == 3, p_group * w2, 0.0))))
    o_ref[...] = out


def _router(x, wg, bg, we, be, tm=512):
    t, d = x.shape
    tm = min(tm, t)
    w = jnp.concatenate([wg, jnp.transpose(we, (1, 0, 2)).reshape(d, N_EXPERTS)], 1)
    w = jnp.pad(w, ((0, 0), (0, LANES - w.shape[1])))
    bias = jnp.pad(jnp.concatenate([bg, be.reshape(-1)]), (0, LANES - N_GROUPS - N_EXPERTS)).reshape(1, LANES)
    wh = w.astype(BF16)
    wm = (w - wh.astype(F32)).astype(BF16)
    fix = lambda i: (0, 0)
    return pl.pallas_call(
        _router_kernel,
        name="router",
        grid=(t // tm,),
        in_specs=[pl.BlockSpec((tm, d), lambda i: (i, 0)), pl.BlockSpec((d, LANES), fix),
                  pl.BlockSpec((d, LANES), fix), pl.BlockSpec((1, LANES), fix)],
        out_specs=pl.BlockSpec((tm, LANES), lambda i: (i, 0)),
        out_shape=jax.ShapeDtypeStruct((t, LANES), F32),
        compiler_params=_params("parallel"),
    )(x, wh, wm, bias)


def _ffn_kernel(bexp_ref, nvalid_ref, pk_ref, x_hbm, w1_ref, w3_ref, w2_ref, y_hbm,
                xbuf, ybuf, gsem, ssem, *, bm):
    j = pl.program_id(0)
    nb = pl.num_programs(0)
    nvalid = nvalid_ref[0]
    slot = j % 2

    def row_copy_in(blk, s, r):
        src = pk_ref[blk * bm + r] & ((1 << TOK_BITS) - 1)
        return pltpu.make_async_copy(x_hbm.at[pl.ds(src, 1)], xbuf.at[s, pl.ds(r, 1)], gsem.at[s])

    def row_copy_out(blk, s, r):
        dst = lax.shift_right_logical(pk_ref[blk * bm + r], TOK_BITS)
        return pltpu.make_async_copy(ybuf.at[s, pl.ds(r, 1)], y_hbm.at[pl.ds(dst, 1)], ssem.at[s])

    def for_rows(fn):
        def body(r, c):
            fn(r)
            return c
        lax.fori_loop(0, bm, body, 0)

    @pl.when((j == 0) & (nvalid > 0))
    def _():
        for_rows(lambda r: row_copy_in(0, 0, r).start())

    @pl.when(j + 1 < nvalid)
    def _():
        for_rows(lambda r: row_copy_in(j + 1, 1 - slot, r).start())

    def wait_gather(s):
        pltpu.make_async_copy(xbuf.at[s], xbuf.at[s], gsem.at[s]).wait()

    def wait_scatter(s):
        pltpu.make_async_copy(ybuf.at[s], ybuf.at[s], ssem.at[s]).wait()

    @pl.when((j >= 2) & (j - 2 < nvalid))
    def _():
        wait_scatter(slot)

    @pl.when(j < nvalid)
    def _():
        wait_gather(slot)
        xb = xbuf[slot]
        h = _silu(_dot(xb, w1_ref[...])) * _dot(xb, w3_ref[...])
        ybuf[slot] = _dot(h, w2_ref[...])
        for_rows(lambda r: row_copy_out(j, slot, r).start())

    @pl.when(j >= nvalid)
    def _():
        ybuf[slot] = jnp.zeros((bm, ybuf.shape[2]), F32)
        fill = pltpu.make_async_copy(ybuf.at[slot], y_hbm.at[pl.ds(j * bm, bm)], ssem.at[slot])
        fill.start()
        fill.wait()

    @pl.when(j == nb - 1)
    def _():
        @pl.when((j >= 1) & (j - 1 < nvalid))
        def _():
            wait_scatter(1 - slot)

        @pl.when(j < nvalid)
        def _():
            wait_scatter(slot)


def _moe_dispatch(slab, t, bm):
    n = 2 * t
    e = N_EXPERTS
    rows = -(-(n + e * (bm - 1)) // bm) * bm
    ids = slab[:, :2].astype(jnp.int32).reshape(n)
    onehot = (ids[:, None] == jnp.arange(e, dtype=jnp.int32)[None, :]).astype(jnp.int32)
    csum = jnp.cumsum(onehot, 0)
    rank = jnp.sum((csum - onehot) * onehot, 1)
    counts = csum[-1]
    padded = -(-counts // bm) * bm
    pend = jnp.cumsum(padded)
    pstart = pend - padded
    dest = pstart[ids] + rank
    a = jnp.arange(n, dtype=jnp.int32)
    tok = a // 2
    out_row = (a % 2) * t + tok
    valid = jnp.zeros((rows,), jnp.int32).at[dest].set(1)
    pad_rank = jnp.cumsum(1 - valid) - (1 - valid)
    row_src = jnp.zeros((rows,), jnp.int32).at[dest].set(tok)
    row_dst = (n + pad_rank).astype(jnp.int32).at[dest].set(out_row)
    packed = row_src | (row_dst << TOK_BITS)
    n_blocks = rows // bm
    bstart = jnp.arange(n_blocks, dtype=jnp.int32) * bm
    block_exp = jnp.clip(jnp.searchsorted(pend, bstart, side='right'), 0, e - 1).astype(jnp.int32)
    nvalid = (pend[-1] // bm).astype(jnp.int32).reshape(1)
    return packed, block_exp, nvalid, rows


def _moe_ffn(x, slab, w1, w3, w2, bm=MOE_BM):
    t, d = x.shape
    de = w1.shape[2]
    assert t <= (1 << TOK_BITS)
    packed, block_exp, nvalid, rows = _moe_dispatch(slab, t, bm)
    assert rows < (1 << (32 - TOK_BITS))
    n_blocks = rows // bm
    wmap = lambda j, bexp, nv, pk: (bexp[j], 0, 0)
    grid_spec = pltpu.PrefetchScalarGridSpec(
        num_scalar_prefetch=3,
        grid=(n_blocks,),
        in_specs=[pl.BlockSpec(memory_space=pl.ANY),
                  pl.BlockSpec((None, d, de), wmap), pl.BlockSpec((None, d, de), wmap),
                  pl.BlockSpec((None, de, d), wmap)],
        out_specs=pl.BlockSpec(memory_space=pl.ANY),
        scratch_shapes=[pltpu.VMEM((2, bm, d), F32), pltpu.VMEM((2, bm, d), F32),
                        pltpu.SemaphoreType.DMA((2,)), pltpu.SemaphoreType.DMA((2,))],
    )
    return pl.pallas_call(
        functools.partial(_ffn_kernel, bm=bm),
        name="moe_ffn",
        grid_spec=grid_spec,
        out_shape=jax.ShapeDtypeStruct((rows, d), F32),
        compiler_params=_params("arbitrary"),
    )(block_exp, nvalid, packed, x, w1.astype(BF16), w3.astype(BF16), w2.astype(BF16))


def _moe_ln_kernel(x_ref, y0_ref, y1_ref, s_ref, g_ref, b_ref, o_ref, *, alpha):
    s = s_ref[...]
    y = alpha * x_ref[...] + s[:, 2:3] * y0_ref[...] + s[:, 3:4] * y1_ref[...]
    o_ref[...] = _layer_norm(y, g_ref[...], b_ref[...])


def _moe_ln(x, y, slab, g, b, alpha, tm=512):
    t, d = x.shape
    tm = min(tm, t)
    nt = t // tm
    row = lambda i: (i, 0)
    fix = lambda i: (0, 0)
    return pl.pallas_call(
        functools.partial(_moe_ln_kernel, alpha=alpha),
        name="moe_ln",
        grid=(nt,),
        in_specs=[pl.BlockSpec((tm, d), row), pl.BlockSpec((tm, d), row),
                  pl.BlockSpec((tm, d), lambda i: (i + nt, 0)), pl.BlockSpec((tm, LANES), row),
                  pl.BlockSpec((1, d), fix), pl.BlockSpec((1, d), fix)],
        out_specs=pl.BlockSpec((tm, d), row),
        out_shape=jax.ShapeDtypeStruct((t, d), F32),
        compiler_params=_params("parallel"),
    )(x, y, y, slab, g.reshape(1, d), b.reshape(1, d))


def _moe_sublayer(x, wg, bg, we, be, w1, w3, w2, g, b, alpha):
    slab = _router(x, wg, bg, we, be)
    y = _moe_ffn(x, slab, w1, w3, w2)
    return _moe_ln(x, y, slab, g, b, alpha)


def _attn_kernel(q_ref, kp_ref, kc_ref, vp_ref, vc_ref, o_ref, k_all, v_all, q_all, acc_o, acc_m, acc_l, *, rows):
    n = pl.program_id(2)
    k_all[0:rows, :] = kp_ref[...]
    k_all[rows:2 * rows, :] = kc_ref[...]
    v_all[0:rows, :] = vp_ref[...]
    v_all[rows:2 * rows, :] = vc_ref[...]
    q_all[...] = q_ref[...] * (HEAD_DIM ** -0.5)
    nk = ATT_BLOCK
    qi = lax.broadcasted_iota(jnp.int32, (nk, 2 * nk), 0)
    kj = lax.broadcasted_iota(jnp.int32, (nk, 2 * nk), 1)
    band = (kj >= qi) & (kj <= qi + nk)
    lane = lax.broadcasted_iota(jnp.int32, (nk, LANES), 1)
    head1 = lane >= HEAD_DIM

    for p_idx, (window, dil) in enumerate(C_PAIRS):
        assert window // dil == nk
        span = nk * dil
        n_sub = rows // span

        def unit(u, carry, dil=dil, span=span, n_sub=n_sub, first=(p_idx == 0)):
            res = u // n_sub
            sub = u - res * n_sub
            qstart = sub * span + res
            sel = lambda ref, start: ref[pl.ds(start, nk, stride=dil), :]
            q = sel(q_all, qstart)
            k2 = jnp.concatenate([sel(k_all, rows + qstart - span), sel(k_all, rows + qstart)], 0).astype(BF16)
            v2 = jnp.concatenate([sel(v_all, rows + qstart - span), sel(v_all, rows + qstart)], 0).astype(BF16)
            ok = band & ((kj >= nk) | (n > 0) | (sub > 0))
            outs = []
            for h1 in (False, True):
                hm = head1 if h1 else jnp.logical_not(head1)
                sc = _dot(jnp.where(hm, q, 0.0), k2, NT)
                sc = jnp.where(ok, sc, -jnp.inf)
                m = jnp.max(sc, -1, keepdims=True)
                p = jnp.exp(sc - m)
                outs.append((_dot(p, v2), m, jnp.sum(p, -1, keepdims=True)))
            o = jnp.where(head1, outs[1][0], outs[0][0])
            m = jnp.where(head1, outs[1][1], outs[0][1])
            l = jnp.where(head1, outs[1][2], outs[0][2])
            dst = pl.ds(qstart, nk, stride=dil)
            if not first:
                m_old = acc_m[dst, :]
                m_new = jnp.maximum(m_old, m)
                a_old = jnp.exp(m_old - m_new)
                a_cur = jnp.exp(m - m_new)
                o = acc_o[dst, :] * a_old + o * a_cur
                l = acc_l[dst, :] * a_old + l * a_cur
                m = m_new
            acc_o[dst, :] = o
            acc_m[dst, :] = m
            acc_l[dst, :] = l
            return carry

        lax.fori_loop(0, dil * n_sub, unit, 0)

    o_ref[...] = acc_o[...] / acc_l[...]


def _dilated_attention(p, b, s, qoff, koff, voff):
    rows = max(w for w, _ in C_PAIRS)
    assert s % rows == 0
    nblk = s // rows
    cur = lambda off: pl.BlockSpec((rows, LANES), lambda bi, hp, n: (bi * nblk + n, off + hp))
    prev = lambda off: pl.BlockSpec((rows, LANES), lambda bi, hp, n: (bi * nblk + jnp.maximum(n - 1, 0), off + hp))
    scratch = [pltpu.VMEM((2 * rows, LANES), F32), pltpu.VMEM((2 * rows, LANES), F32)] + \
              [pltpu.VMEM((rows, LANES), F32)] * 4
    return pl.pallas_call(
        functools.partial(_attn_kernel, rows=rows),
        name="dil_attn",
        grid=(b, C_WIDTH // LANES, nblk),
        in_specs=[cur(qoff), prev(koff), cur(koff), prev(voff), cur(voff)],
        out_specs=pl.BlockSpec((rows, LANES), lambda bi, hp, n: (bi * nblk + n, hp)),
        out_shape=jax.ShapeDtypeStruct((b * s, C_WIDTH), F32),
        scratch_shapes=scratch,
        compiler_params=_params("parallel", "parallel", "arbitrary"),
    )(p, p, p, p, p)


def _causal_conv(x_ref, pad_ref, w_ref, b_ref, first, blk):
    @pl.when(first)
    def _():
        pad_ref[0:8, :] = jnp.zeros((8, LANES), F32)
    pad_ref[8:8 + blk, :] = x_ref[...]
    w = w_ref[...]
    acc = b_ref[...]
    for j in range(D_CONV):
        acc = acc + w[j:j + 1, :] * pad_ref[pl.ds(8 - (D_CONV - 1) + j, blk), :]
    pad_ref[0:8, :] = pad_ref[blk:blk + 8, :]
    return acc


def _mlstm_kernel(q_ref, k_ref, v_ref, og_ref, gate_ref, cwq_ref, cwk_ref, cbq_ref, cbk_ref, gb_ref,
                  o_ref, qpad, kpad, c_state, n_state, m_state, *, blk, chunk):
    h = pl.program_id(1)
    first = pl.program_id(2) == 0

    @pl.when(first)
    def _():
        c_state[...] = jnp.zeros_like(c_state)
        n_state[...] = jnp.zeros_like(n_state)
        m_state[...] = jnp.zeros_like(m_state)

    qc = _silu(_causal_conv(q_ref, qpad, cwq_ref, cbq_ref, first, blk))
    kc = _silu(_causal_conv(k_ref, kpad, cwk_ref, cbk_ref, first, blk)) * (D_DQK ** -0.5)
    gates = gate_ref[...] + gb_ref[...]
    logf = _log_sigmoid(gates)
    lane = lax.broadcasted_iota(jnp.int32, (chunk, LANES), 1)
    sel_i = lane == h
    sel_f = lane == D_HEADS + h
    ti = lax.broadcasted_iota(jnp.int32, (chunk, chunk), 0)
    si = lax.broadcasted_iota(jnp.int32, (chunk, chunk), 1)
    causal = si <= ti
    tri = causal.astype(F32)

    c_mat = c_state[...]
    n_vec = n_state[...]
    m_prev = m_state[...][:, 0:1]
    for c in range(blk // chunk):
        rows = slice(c * chunk, (c + 1) * chunk)
        q_c, k_c, v_c = qc[rows], kc[rows], v_ref[rows, :]
        gi = gates[rows]
        fcum_all = _dot_exact_lhs(tri, logf[rows])
        f_col = jnp.sum(jnp.where(sel_f, fcum_all, 0.0), -1, keepdims=True)
        i_col = jnp.sum(jnp.where(sel_i, gi, 0.0), -1, keepdims=True)
        f_row = _dot_exact_lhs(sel_f.astype(F32), fcum_all, NT)
        i_row = _dot_exact_lhs(sel_i.astype(F32), gi, NT)
        log_d = jnp.where(causal, f_col - f_row + i_row, -jnp.inf)
        log_inter = f_col + m_prev
        m_t = jnp.maximum(jnp.max(log_d, -1, keepdims=True), log_inter)
        w_inter = jnp.exp(log_inter - m_t)
        scores = _dot(q_c, k_c, NT) * jnp.exp(log_d - m_t)
        num = w_inter * _dot(q_c, c_mat) + _dot(scores, v_c)
        den = w_inter * jnp.sum(q_c * n_vec, -1, keepdims=True) + jnp.sum(scores, -1, keepdims=True)
        out = num / jnp.maximum(jnp.abs(den), jnp.exp(-m_t))
        o_ref[rows, :] = _sigmoid(og_ref[rows, :]) * out
        f_tot = f_col[chunk - 1:chunk]
        log_w = f_tot - f_col + i_col
        m_new = jnp.maximum(f_tot + m_prev, jnp.max(log_w, 0, keepdims=True))
        kw = k_c * jnp.exp(log_w - m_new)
        w_old = jnp.exp(f_tot + m_prev - m_new)
        c_mat = w_old * c_mat + _dot(kw.T, v_c)
        n_vec = w_old * n_vec + jnp.sum(kw, 0, keepdims=True)
        m_prev = m_new
    c_state[...] = c_mat
    n_state[...] = n_vec
    m_state[...] = jnp.broadcast_to(m_prev, m_state.shape)


def _mlstm(p, b, s, off, conv_w, conv_b, b_i, b_f, blk=SEQ_BLOCK, chunk=ML_CHUNK):
    blk = min(blk, s)
    nblk = s // blk
    hq = D_QK_WIDTH // LANES
    vw = D_DV // LANES
    assert off % vw == 0
    rowblk = lambda bi, h, n: bi * nblk + n
    qk_spec = lambda o: pl.BlockSpec((blk, LANES), lambda bi, h, n: (rowblk(bi, h, n), off + o + h))
    v_spec = lambda o: pl.BlockSpec((blk, D_DV), lambda bi, h, n: (rowblk(bi, h, n), (off + o) // vw + h))
    par_spec = lambda rows, o: pl.BlockSpec((rows, LANES), lambda bi, h, n: (0, o + h))
    gate_off = off + 2 * hq + 2 * D_WIDTH // LANES
    gbias = jnp.pad(jnp.concatenate([b_i, b_f]), (0, LANES - 2 * D_HEADS)).reshape(1, LANES)
    cb = conv_b.reshape(1, -1)
    return pl.pallas_call(
        functools.partial(_mlstm_kernel, blk=blk, chunk=chunk),
        name="mlstm",
        grid=(b, D_HEADS, nblk),
        in_specs=[qk_spec(0), qk_spec(hq), v_spec(2 * hq), v_spec(2 * hq + D_WIDTH // LANES),
                  pl.BlockSpec((blk, LANES), lambda bi, h, n: (rowblk(bi, h, n), gate_off)),
                  par_spec(D_CONV, 0), par_spec(D_CONV, hq), par_spec(1, 0), par_spec(1, hq),
                  pl.BlockSpec((1, LANES), lambda bi, h, n: (0, 0))],
        out_specs=pl.BlockSpec((blk, D_DV), lambda bi, h, n: (rowblk(bi, h, n), h)),
        out_shape=jax.ShapeDtypeStruct((b * s, D_WIDTH), F32),
        scratch_shapes=[pltpu.VMEM((blk + 8, LANES), F32), pltpu.VMEM((blk + 8, LANES), F32),
                        pltpu.VMEM((D_DQK, D_DV), F32), pltpu.VMEM((1, LANES), F32), pltpu.VMEM((1, LANES), F32)],
        compiler_params=_params("parallel", "parallel", "arbitrary"),
    )(p, p, p, p, p, conv_w, conv_w, cb, cb, gbias)


TN = (((0,), (0,)), ((), ()))


def _chunk_masks(blk, chunk):
    t = lax.broadcasted_iota(jnp.int32, (blk, blk), 0)
    s = lax.broadcasted_iota(jnp.int32, (blk, blk), 1)
    same = (t // chunk) == (s // chunk)
    return same, same & (s <= t), same & (s < t)


def _hgrn2_kernel(q_ref, f_ref, i_ref, g_ref, lb_ref, nw_ref, o_ref, s_state, *, blk, chunk):
    @pl.when(pl.program_id(2) == 0)
    def _():
        s_state[...] = jnp.zeros_like(s_state)

    lb = lb_ref[...]
    f = f_ref[...]
    q = _silu(q_ref[...])
    v = i_ref[...]
    a = jnp.log(lb)
    bb = jnp.log1p(-lb) + _log_sigmoid(f)
    log_f = jnp.maximum(a, bb) + jnp.log1p(jnp.exp(-jnp.abs(a - bb)))
    k = (1.0 - lb) * _sigmoid(-f)

    same, tri_incl, _ = _chunk_masks(blk, chunk)
    bcum = _dot_exact_lhs(tri_incl.astype(F32), log_f)
    blast = _dot_exact_lhs(same.astype(F32), log_f)
    qe = q * jnp.exp(bcum)
    kd = k * jnp.exp(blast - bcum)
    decay = jnp.exp(blast)

    pos = lax.broadcasted_iota(jnp.int32, (blk, LANES), 0) % chunk
    o = jnp.sum(q * k, -1, keepdims=True) * v
    for dist in range(1, chunk):
        e = jnp.exp(jnp.where(pos >= dist, bcum - pltpu.roll(bcum, dist, 0), -jnp.inf))
        w = jnp.sum(q * pltpu.roll(k, dist, 0) * e, -1, keepdims=True)
        o = o + w * pltpu.roll(v, dist, 0)

    st = s_state[...]
    inter = []
    for c in range(blk // chunk):
        rows = slice(c * chunk, (c + 1) * chunk)
        inter.append(_dot(qe[rows], st, NT))
        st = st * decay[c * chunk:c * chunk + 1] + _dot(v[rows], kd[rows], TN)
    s_state[...] = st
    o = o + jnp.concatenate(inter, 0)
    o = o * lax.rsqrt(jnp.mean(o * o, -1, keepdims=True) + RMS_EPS) * nw_ref[...]
    o_ref[...] = o * _silu(g_ref[...])


def _hgrn2(p, b, s, off, lb, norm_w, blk=SEQ_BLOCK, chunk=GLA_CHUNK):
    blk = min(blk, s)
    nblk = s // blk
    spec = lambda o: pl.BlockSpec((blk, LANES), lambda bi, h, n: (bi * nblk + n, off + o * B_HEADS + h))
    return pl.pallas_call(
        functools.partial(_hgrn2_kernel, blk=blk, chunk=chunk),
        name="hgrn2",
        grid=(b, B_HEADS, nblk),
        in_specs=[spec(0), spec(1), spec(2), spec(3),
                  pl.BlockSpec((1, LANES), lambda bi, h, n: (0, h)),
                  pl.BlockSpec((1, LANES), lambda bi, h, n: (0, 0))],
        out_specs=pl.BlockSpec((blk, LANES), lambda bi, h, n: (bi * nblk + n, h)),
        out_shape=jax.ShapeDtypeStruct((b * s, B_WIDTH), F32),
        scratch_shapes=[pltpu.VMEM((B_DV, B_DK), F32)],
        compiler_params=_params("parallel", "parallel", "arbitrary"),
    )(p, p, p, p, lb.reshape(1, -1), norm_w.reshape(1, -1))


def _rwkv7_kernel(r_ref, k_ref, v_ref, wa_ref, xg_ref, mur_ref, muk_ref, muv_ref, muwa_ref, mug_ref,
                  w0_ref, a0_ref, kk_ref, ka_ref, rk_ref, lnw_ref, lnb_ref, w2_ref, a2_ref, g2_ref,
                  o_ref, tail, s_state, *, blk, chunk):
    @pl.when(pl.program_id(2) == 0)
    def _():
        tail[...] = jnp.zeros_like(tail)
        s_state[...] = jnp.zeros_like(s_state)

    row0 = lax.broadcasted_iota(jnp.int32, (blk, LANES), 0) == 0

    def shift_lerp(idx, x_ref, mu_ref):
        x = x_ref[...]
        prev = tail[0:1, idx * LANES:(idx + 1) * LANES]
        shifted = jnp.where(row0, prev, pltpu.roll(x, 1, 0))
        tail[0:1, idx * LANES:(idx + 1) * LANES] = x[blk - 1:blk]
        return x + (shifted - x) * mu_ref[...]

    r = shift_lerp(0, r_ref, mur_ref)
    k = shift_lerp(1, k_ref, muk_ref)
    v = shift_lerp(2, v_ref, muv_ref)
    wa = shift_lerp(3, wa_ref, muwa_ref)
    xg = shift_lerp(4, xg_ref, mug_ref)

    lane = lax.broadcasted_iota(jnp.int32, (LANES, LANES), 1)
    sub = lax.broadcasted_iota(jnp.int32, (LANES, LANES), 0)
    head_bd = (lane // HEAD_DIM) == (sub // HEAD_DIM)
    head_sum = lambda x: _dot_exact_rhs(x, head_bd.astype(F32))

    pre_w = -(w0_ref[...] + _dot(jnp.tanh(wa), w2_ref[...]))
    w_log = -(jnp.maximum(pre_w, 0.0) + jnp.log1p(jnp.exp(-jnp.abs(pre_w)))) - 0.5
    lw = -jnp.exp(w_log)
    a = _sigmoid(a0_ref[...] + _dot(wa, a2_ref[...]))
    g = _dot(_sigmoid(xg), g2_ref[...])
    kk = k * kk_ref[...]
    kk = kk / jnp.maximum(jnp.sqrt(head_sum(kk * kk)), 1e-12)
    k = k * (1.0 + (a - 1.0) * ka_ref[...])
    bonus = head_sum(r * k * rk_ref[...]) * v
    av = -kk
    bv = kk * a

    same, tri_incl, _ = _chunk_masks(blk, chunk)
    lcum = _dot_exact_lhs(tri_incl.astype(F32), lw)
    lsum = _dot_exact_lhs(same.astype(F32), lw)
    a_t = av * jnp.exp(lcum - lw)
    r_t = r * jnp.exp(lcum)
    inv = jnp.exp(-lcum)
    b_t = bv * inv
    k_t = k * inv
    to_end = jnp.exp(lsum - lcum)
    b_c = bv * to_end
    k_c = k * to_end
    gam = jnp.exp(lsum)

    c2 = 2 * chunk
    lane_c = lax.broadcasted_iota(jnp.int32, (chunk, LANES), 1)
    h1 = lane_c >= HEAD_DIM
    h0 = jnp.logical_not(h1)
    stack = lambda x: jnp.concatenate([jnp.where(h0, x, 0.0), jnp.where(h1, x, 0.0)], 0)
    ri = lax.broadcasted_iota(jnp.int32, (2 * c2, 2 * c2), 0)
    ci = lax.broadcasted_iota(jnp.int32, (2 * c2, 2 * c2), 1)
    amask = ((ci % chunk) < (ri % chunk)) | (((ci % chunk) == (ri % chunk)) & (ri >= c2))
    eye = (lax.broadcasted_iota(jnp.int32, (c2, c2), 0) == lax.broadcasted_iota(jnp.int32, (c2, c2), 1)).astype(F32)

    st = s_state[...]
    ys = []
    for c in range(blk // chunk):
        rows = slice(c * chunk, (c + 1) * chunk)
        ar = jnp.concatenate([stack(a_t[rows]), stack(r_t[rows])], 0)
        bk = jnp.concatenate([stack(b_t[rows]), stack(k_t[rows])], 0)
        amat = jnp.where(amask, _dot(ar, bk, NT), 0.0)
        n_mat = amat[:c2, :c2]
        t_inv = eye + n_mat
        x = n_mat
        for _ in range(chunk.bit_length() - 2):
            x = _dot(x, x)
            t_inv = t_inv + _dot(t_inv, x)
        v_c = v[rows]
        v2 = jnp.concatenate([v_c, v_c], 0)
        g1 = _dot(ar, st, NT)
        u2 = _dot(t_inv, g1[:c2] + _dot(amat[:c2, c2:], v2))
        y2 = g1[c2:] + _dot(amat[c2:], jnp.concatenate([u2, v2], 0))
        ys.append(jnp.where(h1, y2[chunk:], y2[:chunk]))
        u = jnp.where(h1, u2[chunk:], u2[:chunk])
        upd = _dot(jnp.concatenate([u, v_c], 0), jnp.concatenate([b_c[rows], k_c[rows]], 0), TN)
        st = jnp.where(head_bd, st * gam[c * chunk:c * chunk + 1] + upd, 0.0)
    s_state[...] = st

    y = jnp.concatenate(ys, 0)
    mu = head_sum(y) * (1.0 / HEAD_DIM)
    yc = y - mu
    var = head_sum(yc * yc) * (1.0 / HEAD_DIM)
    y = yc * lax.rsqrt(var + A_GN_EPS) * lnw_ref[...] + lnb_ref[...] + bonus
    o_ref[...] = y * g


def _rwkv7(p, b, s, mu, w0, w2, a0, a2, g2, k_k, k_a, r_k, lnx_w, lnx_b, blk=SEQ_BLOCK, chunk=RW_CHUNK):
    blk = min(blk, s)
    nblk = s // blk
    npair = A_WIDTH // LANES
    lora = 3 * npair
    act = lambda o: pl.BlockSpec((blk, LANES), lambda bi, hp, n: (bi * nblk + n, o + hp))
    act_fix = lambda o: pl.BlockSpec((blk, LANES), lambda bi, hp, n: (bi * nblk + n, o))
    vec = lambda o: pl.BlockSpec((1, LANES), lambda bi, hp, n: (0, o + hp))
    vec_fix = lambda o: pl.BlockSpec((1, LANES), lambda bi, hp, n: (0, o))
    mat = pl.BlockSpec((LANES, LANES), lambda bi, hp, n: (0, hp))
    row = lambda t: t.reshape(1, -1)
    w2p = jnp.concatenate([w2, jnp.zeros_like(a2)], 0)
    a2p = jnp.concatenate([jnp.zeros_like(w2), a2], 0)
    mu2 = row(mu)
    return pl.pallas_call(
        functools.partial(_rwkv7_kernel, blk=blk, chunk=chunk),
        name="rwkv7",
        grid=(b, npair, nblk),
        in_specs=[act(0), act(npair), act(2 * npair), act_fix(lora), act_fix(lora + 1),
                  vec(0), vec(npair), vec(2 * npair), vec_fix(lora), vec_fix(lora + 1),
                  vec(0), vec(0), vec(0), vec(0), vec(0), vec(0), vec(0), mat, mat, mat],
        out_specs=pl.BlockSpec((blk, LANES), lambda bi, hp, n: (bi * nblk + n, hp)),
        out_shape=jax.ShapeDtypeStruct((b * s, A_WIDTH), F32),
        scratch_shapes=[pltpu.VMEM((8, 5 * LANES), F32), pltpu.VMEM((LANES, LANES), F32)],
        compiler_params=_params("parallel", "parallel", "arbitrary"),
    )(p, p, p, p, p, mu2, mu2, mu2, mu2, mu2,
      row(w0), row(a0), row(k_k), row(k_a), row(r_k), row(lnx_w), row(lnx_b), w2p, a2p, g2)


def kernel(x, ev_w_in, ev_w_out, rwkv_mu, rwkv_w0, rwkv_w2, rwkv_a0, rwkv_a2, rwkv_g2, rwkv_kk, rwkv_ka, rwkv_rk,
           rwkv_lnx_w, rwkv_lnx_b, hgrn_lb, hgrn_norm_w, od_w_in, od_w_out, mlstm_conv_w, mlstm_conv_b,
           mlstm_b_i, mlstm_b_f, ln_w, ln_b, moe_wg, moe_bg, moe_we, moe_be, moe_w1, moe_w3, moe_w2):
    b, s, d = x.shape
    depth = ln_w.shape[0]
    alpha = (2 * depth) ** 0.25
    lb_all = jnp.cumsum(jax.nn.softmax(hgrn_lb.astype(F32), axis=0), axis=0)
    lb_all = lb_all - lb_all[:1]
    tn = 2 * LANES
    xf = x.reshape(b * s, d)
    for layer in range(depth):
        j = layer // 2
        if layer % 2 == 0:
            p = _matmul(xf, ev_w_in[j].astype(BF16), tn=tn)
            ya = _rwkv7(p, b, s, rwkv_mu[j], rwkv_w0[j], rwkv_w2[j], rwkv_a0[j], rwkv_a2[j], rwkv_g2[j],
                        rwkv_kk[j], rwkv_ka[j], rwkv_rk[j], rwkv_lnx_w[j], rwkv_lnx_b[j])
            yb = _hgrn2(p, b, s, A_COLS // LANES, lb_all[j], hgrn_norm_w[j])
            w_out = ev_w_out[j]
        else:
            w_in = od_w_in[j]
            w_in = jnp.pad(w_in, ((0, 0), (0, -w_in.shape[1] % tn))).astype(BF16)
            p = _matmul(xf, w_in, tn=tn)
            nq = C_WIDTH // LANES
            ya = _dilated_attention(p, b, s, 0, nq, 2 * nq)
            yb = _mlstm(p, b, s, 3 * nq, mlstm_conv_w[j], mlstm_conv_b[j], mlstm_b_i[j], mlstm_b_f[j])
            w_out = od_w_out[j]
        xf = _proj_ln(ya, yb, w_out, xf, ln_w[layer, 0], ln_b[layer, 0], alpha)
        xf = _moe_sublayer(xf, moe_wg[layer], moe_bg[layer], moe_we[layer], moe_be[layer],
                           moe_w1[layer], moe_w3[layer], moe_w2[layer], ln_w[layer, 1], ln_b[layer, 1], alpha)
    return xf.reshape(b, s, d)
```

```python
import functools

import jax
import jax.numpy as jnp
from jax import lax
from jax.experimental import pallas as pl
from jax.experimental.pallas import tpu as pltpu

F32 = jnp.float32
BF16 = jnp.bfloat16

HEAD_DIM = 64
A_HEADS = 16
A_WIDTH = A_HEADS * HEAD_DIM
A_LORA_W = 64
A_LORA_A = 64
A_LORA_G = 128
A_COLS = 3 * A_WIDTH + A_LORA_W + A_LORA_A + A_LORA_G
A_GN_EPS = 64e-5
B_HEADS = 8
B_DK = 128
B_DV = 128
B_WIDTH = B_HEADS * B_DV
C_HEADS = 16
C_WIDTH = C_HEADS * HEAD_DIM
C_PAIRS = ((128, 1), (512, 4), (2048, 16))
D_HEADS = 4
D_DQK = 128
D_DV = 256
D_QK_WIDTH = D_HEADS * D_DQK
D_WIDTH = D_HEADS * D_DV
D_CONV = 4
ODD_COLS = 3 * C_WIDTH + 2 * D_QK_WIDTH + 2 * D_WIDTH + 2 * D_HEADS
N_GROUPS = 4
EXPERTS_PER_GROUP = 8
N_EXPERTS = N_GROUPS * EXPERTS_PER_GROUP
LN_EPS = 1e-5
RMS_EPS = 1e-6

LANES = 128
VMEM_LIMIT = 56 * 1024 * 1024
RW_CHUNK = 64
RW_PAIRS = 4
GLA_CHUNK = 16
ML_CHUNK = 64
SEQ_BLOCK = 256
ATT_BLOCK = 128
ATT_GROUP = 8
MOE_BM = 256
TOK_BITS = 15

NN = (((1,), (0,)), ((), ()))
NT = (((1,), (1,)), ((), ()))


def _dot(a, b, dims=NN):
    return lax.dot_general(a.astype(BF16), b.astype(BF16), dims, preferred_element_type=F32)


def _split3(x):
    hi = x.astype(BF16)
    r1 = x - hi.astype(F32)
    mid = r1.astype(BF16)
    lo = (r1 - mid.astype(F32)).astype(BF16)
    return hi, mid, lo


def _dot_exact_lhs(mask, x, dims=NN):
    hi, mid, lo = _split3(x)
    m = mask.astype(BF16)
    d = lambda p: lax.dot_general(m, p, dims, preferred_element_type=F32)
    return d(hi) + d(mid) + d(lo)


def _dot_exact_rhs(x, mask):
    hi, mid, lo = _split3(x)
    m = mask.astype(BF16)
    d = lambda p: lax.dot_general(p, m, NN, preferred_element_type=F32)
    return d(hi) + d(mid) + d(lo)


def _sigmoid(x):
    return 1.0 / (1.0 + jnp.exp(-x))


def _silu(x):
    return x * _sigmoid(x)


def _log_sigmoid(x):
    return jnp.minimum(x, 0.0) - jnp.log1p(jnp.exp(-jnp.abs(x)))


def _params(*sem):
    return pltpu.CompilerParams(dimension_semantics=sem, vmem_limit_bytes=VMEM_LIMIT)


def _mm_kernel(x_ref, w_ref, o_ref):
    o_ref[...] = _dot(x_ref[...], w_ref[...])


def _matmul(x, w, tm=2048, tn=256):
    t, k = x.shape
    n = w.shape[1]
    tm = min(tm, t)
    return pl.pallas_call(
        _mm_kernel,
        name="in_proj",
        grid=(t // tm, n // tn),
        in_specs=[pl.BlockSpec((tm, k), lambda i, j: (i, 0)),
                  pl.BlockSpec((k, tn), lambda i, j: (0, j))],
        out_specs=pl.BlockSpec((tm, tn), lambda i, j: (i, j)),
        out_shape=jax.ShapeDtypeStruct((t, n), F32),
        compiler_params=_params("parallel", "arbitrary"),
    )(x, w)


def _layer_norm(y, w, b):
    mu = jnp.mean(y, -1, keepdims=True)
    yc = y - mu
    var = jnp.mean(yc * yc, -1, keepdims=True)
    return yc * lax.rsqrt(var + LN_EPS) * w + b


def _proj_ln_kernel(ya_ref, yb_ref, wa_ref, wb_ref, x_ref, g_ref, b_ref, o_ref, *, alpha):
    y = _dot(ya_ref[...], wa_ref[...]) + _dot(yb_ref[...], wb_ref[...]) + alpha * x_ref[...]
    o_ref[...] = _layer_norm(y, g_ref[...], b_ref[...])


def _proj_ln(ya, yb, w_out, x, g, b, alpha, tm=256):
    t, d = x.shape
    ka = ya.shape[1]
    kb = yb.shape[1]
    tm = min(tm, t)
    wa = w_out[:ka].astype(BF16)
    wb = w_out[ka:].astype(BF16)
    row = lambda i: (i, 0)
    fix = lambda i: (0, 0)
    return pl.pallas_call(
        functools.partial(_proj_ln_kernel, alpha=alpha),
        name="proj_ln",
        grid=(t // tm,),
        in_specs=[pl.BlockSpec((tm, ka), row), pl.BlockSpec((tm, kb), row),
                  pl.BlockSpec((ka, d), fix), pl.BlockSpec((kb, d), fix),
                  pl.BlockSpec((tm, d), row), pl.BlockSpec((1, d), fix), pl.BlockSpec((1, d), fix)],
        out_specs=pl.BlockSpec((tm, d), row),
        out_shape=jax.ShapeDtypeStruct((t, d), F32),
        compiler_params=_params("parallel"),
    )(ya, yb, wa, wb, x, g.reshape(1, d), b.reshape(1, d))


def _router_kernel(x_ref, wh_ref, wm_ref, b_ref, o_ref):
    xh, xm, _ = _split3(x_ref[...])
    wh = wh_ref[...]
    d = lambda a, w: lax.dot_general(a, w, NN, preferred_element_type=F32)
    logits = d(xh, wh) + d(xh, wm_ref[...]) + d(xm, wh) + b_ref[...]
    tm = logits.shape[0]
    lane = lax.broadcasted_iota(jnp.int32, (tm, LANES), 1)
    neg = -jnp.inf
    gl = jnp.where(lane < N_GROUPS, logits, neg)
    gmax = jnp.max(gl, -1, keepdims=True)
    g_top = jnp.min(jnp.where(gl == gmax, lane, LANES), -1, keepdims=True)
    p_group = 1.0 / jnp.sum(jnp.exp(gl - gmax), -1, keepdims=True)
    e_lane = lane - N_GROUPS
    in_group = (e_lane >= 0) & (e_lane < N_EXPERTS) & ((e_lane // EXPERTS_PER_GROUP) == g_top)
    el = jnp.where(in_group, logits, neg)
    v1 = jnp.max(el, -1, keepdims=True)
    i1 = jnp.min(jnp.where(el == v1, lane, LANES), -1, keepdims=True)
    el2 = jnp.where(lane == i1, neg, el)
    v2 = jnp.max(el2, -1, keepdims=True)
    i2 = jnp.min(jnp.where(el2 == v2, lane, LANES), -1, keepdims=True)
    e2 = jnp.exp(v2 - v1)
    w1 = 1.0 / (1.0 + e2)
    w2 = e2 / (1.0 + e2)
    out = jnp.where(lane == 0, (i1 - N_GROUPS).astype(F32),
                    jnp.where(lane == 1, (i2 - N_GROUPS).astype(F32),
                              jnp.where(lane == 2, p_group * w1,
                                        jnp.where(lane == 3, p_group * w2, 0.0))))
    o_ref[...] = out


def _router(x, wg, bg, we, be, tm=512):
    t, d = x.shape
    tm = min(tm, t)
    w = jnp.concatenate([wg, jnp.transpose(we, (1, 0, 2)).reshape(d, N_EXPERTS)], 1)
    w = jnp.pad(w, ((0, 0), (0, LANES - w.shape[1])))
    bias = jnp.pad(jnp.concatenate([bg, be.reshape(-1)]), (0, LANES - N_GROUPS - N_EXPERTS)).reshape(1, LANES)
    wh = w.astype(BF16)
    wm = (w - wh.astype(F32)).astype(BF16)
    fix = lambda i: (0, 0)
    return pl.pallas_call(
        _router_kernel,
        name="router",
        grid=(t // tm,),
        in_specs=[pl.BlockSpec((tm, d), lambda i: (i, 0)), pl.BlockSpec((d, LANES), fix),
                  pl.BlockSpec((d, LANES), fix), pl.BlockSpec((1, LANES), fix)],
        out_specs=pl.BlockSpec((tm, LANES), lambda i: (i, 0)),
        out_shape=jax.ShapeDtypeStruct((t, LANES), F32),
        compiler_params=_params("parallel"),
    )(x, wh, wm, bias)


def _ffn_kernel(bexp_ref, nvalid_ref, pk_ref, x_hbm, w1_ref, w3_ref, w2_ref, y_hbm,
                xbuf, ybuf, gsem, ssem, *, bm):
    j = pl.program_id(0)
    nb = pl.num_programs(0)
    nvalid = nvalid_ref[0]
    slot = j % 2

    def row_copy_in(blk, s, r):
        src = pk_ref[blk * bm + r] & ((1 << TOK_BITS) - 1)
        return pltpu.make_async_copy(x_hbm.at[pl.ds(src, 1)], xbuf.at[s, pl.ds(r, 1)], gsem.at[s])

    def row_copy_out(blk, s, r):
        dst = lax.shift_right_logical(pk_ref[blk * bm + r], TOK_BITS)
        return pltpu.make_async_copy(ybuf.at[s, pl.ds(r, 1)], y_hbm.at[pl.ds(dst, 1)], ssem.at[s])

    def for_rows(fn):
        def body(r, c):
            fn(r)
            return c
        lax.fori_loop(0, bm, body, 0, unroll=8)

    @pl.when((j == 0) & (nvalid > 0))
    def _():
        for_rows(lambda r: row_copy_in(0, 0, r).start())

    @pl.when(j + 1 < nvalid)
    def _():
        for_rows(lambda r: row_copy_in(j + 1, 1 - slot, r).start())

    def wait_gather(s):
        pltpu.make_async_copy(xbuf.at[s], xbuf.at[s], gsem.at[s]).wait()

    def wait_scatter(s):
        pltpu.make_async_copy(ybuf.at[s], ybuf.at[s], ssem.at[s]).wait()

    @pl.when((j >= 2) & (j - 2 < nvalid))
    def _():
        wait_scatter(slot)

    @pl.when(j < nvalid)
    def _():
        wait_gather(slot)
        xb = xbuf[slot]
        h = _silu(_dot(xb, w1_ref[...])) * _dot(xb, w3_ref[...])
        ybuf[slot] = _dot(h, w2_ref[...])
        for_rows(lambda r: row_copy_out(j, slot, r).start())

    @pl.when(j >= nvalid)
    def _():
        ybuf[slot] = jnp.zeros((bm, ybuf.shape[2]), F32)
        fill = pltpu.make_async_copy(ybuf.at[slot], y_hbm.at[pl.ds(j * bm, bm)], ssem.at[slot])
        fill.start()
        fill.wait()

    @pl.when(j == nb - 1)
    def _():
        @pl.when((j >= 1) & (j - 1 < nvalid))
        def _():
            wait_scatter(1 - slot)

        @pl.when(j < nvalid)
        def _():
            wait_scatter(slot)


def _moe_dispatch(slab, t, bm):
    n = 2 * t
    e = N_EXPERTS
    rows = -(-(n + e * (bm - 1)) // bm) * bm
    ids = slab[:, :2].astype(jnp.int32).reshape(n)
    order = jnp.argsort(ids, stable=True).astype(jnp.int32)
    counts = jnp.sum((ids[:, None] == jnp.arange(e, dtype=jnp.int32)[None, :]).astype(jnp.int32), 0)
    padded = -(-counts // bm) * bm
    pend = jnp.cumsum(padded)
    pstart = pend - padded
    start = jnp.cumsum(counts) - counts
    n_blocks = rows // bm
    bstart = jnp.arange(n_blocks, dtype=jnp.int32) * bm
    block_exp = jnp.clip(jnp.searchsorted(pend, bstart, side='right'), 0, e - 1).astype(jnp.int32)
    nvalid = (pend[-1] // bm).astype(jnp.int32).reshape(1)
    per_row = lambda tbl: jnp.repeat(tbl[block_exp], bm)
    local = jnp.arange(rows, dtype=jnp.int32) - per_row(pstart)
    valid = local < per_row(counts)
    a = order[jnp.clip(per_row(start) + local, 0, n - 1)]
    pad_rank = jnp.cumsum(1 - valid.astype(jnp.int32)) - 1
    row_src = jnp.where(valid, a // 2, 0)
    row_dst = jnp.where(valid, (a % 2) * t + a // 2, n + pad_rank)
    packed = row_src | (row_dst << TOK_BITS)
    return packed.astype(jnp.int32), block_exp, nvalid, rows


def _moe_ffn(x, slab, w1, w3, w2, bm=MOE_BM):
    t, d = x.shape
    de = w1.shape[2]
    assert t <= (1 << TOK_BITS)
    packed, block_exp, nvalid, rows = _moe_dispatch(slab, t, bm)
    assert rows < (1 << (32 - TOK_BITS))
    n_blocks = rows // bm
    wmap = lambda j, bexp, nv, pk: (bexp[j], 0, 0)
    grid_spec = pltpu.PrefetchScalarGridSpec(
        num_scalar_prefetch=3,
        grid=(n_blocks,),
        in_specs=[pl.BlockSpec(memory_space=pl.ANY),
                  pl.BlockSpec((None, d, de), wmap), pl.BlockSpec((None, d, de), wmap),
                  pl.BlockSpec((None, de, d), wmap)],
        out_specs=pl.BlockSpec(memory_space=pl.ANY),
        scratch_shapes=[pltpu.VMEM((2, bm, d), F32), pltpu.VMEM((2, bm, d), F32),
                        pltpu.SemaphoreType.DMA((2,)), pltpu.SemaphoreType.DMA((2,))],
    )
    return pl.pallas_call(
        functools.partial(_ffn_kernel, bm=bm),
        name="moe_ffn",
        grid_spec=grid_spec,
        out_shape=jax.ShapeDtypeStruct((rows, d), F32),
        compiler_params=_params("arbitrary"),
    )(block_exp, nvalid, packed, x, w1.astype(BF16), w3.astype(BF16), w2.astype(BF16))


def _moe_ln_kernel(x_ref, y0_ref, y1_ref, s_ref, g_ref, b_ref, o_ref, *, alpha):
    s = s_ref[...]
    y = alpha * x_ref[...] + s[:, 2:3] * y0_ref[...] + s[:, 3:4] * y1_ref[...]
    o_ref[...] = _layer_norm(y, g_ref[...], b_ref[...])


def _moe_ln(x, y, slab, g, b, alpha, tm=512):
    t, d = x.shape
    tm = min(tm, t)
    nt = t // tm
    row = lambda i: (i, 0)
    fix = lambda i: (0, 0)
    return pl.pallas_call(
        functools.partial(_moe_ln_kernel, alpha=alpha),
        name="moe_ln",
        grid=(nt,),
        in_specs=[pl.BlockSpec((tm, d), row), pl.BlockSpec((tm, d), row),
                  pl.BlockSpec((tm, d), lambda i: (i + nt, 0)), pl.BlockSpec((tm, LANES), row),
                  pl.BlockSpec((1, d), fix), pl.BlockSpec((1, d), fix)],
        out_specs=pl.BlockSpec((tm, d), row),
        out_shape=jax.ShapeDtypeStruct((t, d), F32),
        compiler_params=_params("parallel"),
    )(x, y, y, slab, g.reshape(1, d), b.reshape(1, d))


def _moe_sublayer(x, wg, bg, we, be, w1, w3, w2, g, b, alpha):
    slab = _router(x, wg, bg, we, be)
    y = _moe_ffn(x, slab, w1, w3, w2)
    return _moe_ln(x, y, slab, g, b, alpha)


def _attn_kernel(q_ref, kp_ref, kc_ref, vp_ref, vc_ref, o_ref, k_all, v_all, q_all, acc_o, acc_m, acc_l, *, rows):
    n = pl.program_id(2)
    k_all[0:rows, :] = kp_ref[...]
    k_all[rows:2 * rows, :] = kc_ref[...]
    v_all[0:rows, :] = vp_ref[...]
    v_all[rows:2 * rows, :] = vc_ref[...]
    q_all[...] = q_ref[...] * (HEAD_DIM ** -0.5)
    nk = ATT_BLOCK
    qi = lax.broadcasted_iota(jnp.int32, (nk, 2 * nk), 0)
    kj = lax.broadcasted_iota(jnp.int32, (nk, 2 * nk), 1)
    band = (kj >= qi) & (kj <= qi + nk)
    lane = lax.broadcasted_iota(jnp.int32, (nk, LANES), 1)
    head1 = lane >= HEAD_DIM

    for p_idx, (window, dil) in enumerate(C_PAIRS):
        assert window // dil == nk
        span = nk * dil
        n_sub = rows // span

        def group(gi, carry, dil=dil, span=span, n_sub=n_sub, first=(p_idx == 0)):
            sel = lambda ref, start: ref[pl.ds(start, nk, stride=dil), :]
            loaded = []
            for i in range(ATT_GROUP):
                u = gi * ATT_GROUP + i
                res = u // n_sub
                sub = u - res * n_sub
                qstart = sub * span + res
                q = sel(q_all, qstart)
                k2 = jnp.concatenate([sel(k_all, rows + qstart - span), sel(k_all, rows + qstart)], 0).astype(BF16)
                v2 = jnp.concatenate([sel(v_all, rows + qstart - span), sel(v_all, rows + qstart)], 0).astype(BF16)
                dst = pl.ds(qstart, nk, stride=dil)
                old = None if first else (acc_o[dst, :], acc_m[dst, :], acc_l[dst, :])
                loaded.append((q, k2, v2, dst, old, sub))
            results = []
            for q, k2, v2, dst, old, sub in loaded:
                ok = band & ((kj >= nk) | (n > 0) | (sub > 0))
                outs = []
                for h1 in (False, True):
                    hm = head1 if h1 else jnp.logical_not(head1)
                    sc = _dot(jnp.where(hm, q, 0.0), k2, NT)
                    sc = jnp.where(ok, sc, -jnp.inf)
                    m = jnp.max(sc, -1, keepdims=True)
                    p = jnp.exp(sc - m)
                    outs.append((_dot(p, v2), m, jnp.sum(p, -1, keepdims=True)))
                o = jnp.where(head1, outs[1][0], outs[0][0])
                m = jnp.where(head1, outs[1][1], outs[0][1])
                l = jnp.where(head1, outs[1][2], outs[0][2])
                if old is not None:
                    o_old, m_old, l_old = old
                    m_new = jnp.maximum(m_old, m)
                    a_old = jnp.exp(m_old - m_new)
                    a_cur = jnp.exp(m - m_new)
                    o = o_old * a_old + o * a_cur
                    l = l_old * a_old + l * a_cur
                    m = m_new
                results.append((dst, o, m, l))
            for dst, o, m, l in results:
                acc_o[dst, :] = o
                acc_m[dst, :] = m
                acc_l[dst, :] = l
            return carry

        assert (dil * n_sub) % ATT_GROUP == 0
        lax.fori_loop(0, dil * n_sub // ATT_GROUP, group, 0)

    o_ref[...] = acc_o[...] / acc_l[...]


def _dilated_attention(p, b, s, qoff, koff, voff):
    rows = max(w for w, _ in C_PAIRS)
    assert s % rows == 0
    nblk = s // rows
    cur = lambda off: pl.BlockSpec((rows, LANES), lambda bi, hp, n: (bi * nblk + n, off + hp))
    prev = lambda off: pl.BlockSpec((rows, LANES), lambda bi, hp, n: (bi * nblk + jnp.maximum(n - 1, 0), off + hp))
    scratch = [pltpu.VMEM((2 * rows, LANES), F32), pltpu.VMEM((2 * rows, LANES), F32)] + \
              [pltpu.VMEM((rows, LANES), F32)] * 4
    return pl.pallas_call(
        functools.partial(_attn_kernel, rows=rows),
        name="dil_attn",
        grid=(b, C_WIDTH // LANES, nblk),
        in_specs=[cur(qoff), prev(koff), cur(koff), prev(voff), cur(voff)],
        out_specs=pl.BlockSpec((rows, LANES), lambda bi, hp, n: (bi * nblk + n, hp)),
        out_shape=jax.ShapeDtypeStruct((b * s, C_WIDTH), F32),
        scratch_shapes=scratch,
        compiler_params=_params("parallel", "parallel", "arbitrary"),
    )(p, p, p, p, p)


def _causal_conv(x_ref, pad_ref, w_ref, b_ref, first, blk):
    @pl.when(first)
    def _():
        pad_ref[0:8, :] = jnp.zeros((8, pad_ref.shape[1]), F32)
    pad_ref[8:8 + blk, :] = x_ref[...]
    w = w_ref[...]
    acc = b_ref[...]
    for j in range(D_CONV):
        acc = acc + w[j:j + 1, :] * pad_ref[pl.ds(8 - (D_CONV - 1) + j, blk), :]
    pad_ref[0:8, :] = pad_ref[blk:blk + 8, :]
    return acc


def _mlstm_kernel(q_ref, k_ref, v_ref, og_ref, gate_ref, cwq_ref, cwk_ref, cbq_ref, cbk_ref, gb_ref,
                  o_ref, qpad, kpad, c_state, n_state, m_state, *, blk, chunk):
    first = pl.program_id(1) == 0

    @pl.when(first)
    def _():
        c_state[...] = jnp.zeros_like(c_state)
        n_state[...] = jnp.zeros_like(n_state)
        m_state[...] = jnp.zeros_like(m_state)

    qc = _silu(_causal_conv(q_ref, qpad, cwq_ref, cbq_ref, first, blk))
    kc = _silu(_causal_conv(k_ref, kpad, cwk_ref, cbk_ref, first, blk)) * (D_DQK ** -0.5)
    gates = gate_ref[...] + gb_ref[...]
    logf = _log_sigmoid(gates)
    lane = lax.broadcasted_iota(jnp.int32, (chunk, LANES), 1)
    ti = lax.broadcasted_iota(jnp.int32, (chunk, chunk), 0)
    si = lax.broadcasted_iota(jnp.int32, (chunk, chunk), 1)
    causal = si <= ti
    tri = causal.astype(F32)

    c_mat = [c_state[h] for h in range(D_HEADS)]
    n_vec = [n_state[h:h + 1, :] for h in range(D_HEADS)]
    m_prev = [m_state[h:h + 1, 0:1] for h in range(D_HEADS)]
    for c in range(blk // chunk):
        rows = slice(c * chunk, (c + 1) * chunk)
        gi = gates[rows]
        fcum_all = _dot_exact_lhs(tri, logf[rows])
        for h in range(D_HEADS):
            sel_i = lane == h
            sel_f = lane == D_HEADS + h
            qs = slice(h * D_DQK, (h + 1) * D_DQK)
            vs = slice(h * D_DV, (h + 1) * D_DV)
            q_c, k_c, v_c = qc[rows, qs], kc[rows, qs], v_ref[rows, vs]
            f_col = jnp.sum(jnp.where(sel_f, fcum_all, 0.0), -1, keepdims=True)
            i_col = jnp.sum(jnp.where(sel_i, gi, 0.0), -1, keepdims=True)
            f_row = _dot_exact_lhs(sel_f.astype(F32), fcum_all, NT)
            i_row = _dot_exact_lhs(sel_i.astype(F32), gi, NT)
            log_d = jnp.where(causal, f_col - f_row + i_row, -jnp.inf)
            log_inter = f_col + m_prev[h]
            m_t = jnp.maximum(jnp.max(log_d, -1, keepdims=True), log_inter)
            w_inter = jnp.exp(log_inter - m_t)
            scores = _dot(q_c, k_c, NT) * jnp.exp(log_d - m_t)
            num = w_inter * _dot(q_c, c_mat[h]) + _dot(scores, v_c)
            den = w_inter * jnp.sum(q_c * n_vec[h], -1, keepdims=True) + jnp.sum(scores, -1, keepdims=True)
            out = num / jnp.maximum(jnp.abs(den), jnp.exp(-m_t))
            o_ref[rows, vs] = _sigmoid(og_ref[rows, vs]) * out
            f_tot = f_col[chunk - 1:chunk]
            log_w = f_tot - f_col + i_col
            m_new = jnp.maximum(f_tot + m_prev[h], jnp.max(log_w, 0, keepdims=True))
            kw = k_c * jnp.exp(log_w - m_new)
            w_old = jnp.exp(f_tot + m_prev[h] - m_new)
            c_mat[h] = w_old * c_mat[h] + _dot(kw.T, v_c)
            n_vec[h] = w_old * n_vec[h] + jnp.sum(kw, 0, keepdims=True)
            m_prev[h] = m_new
    for h in range(D_HEADS):
        c_state[h] = c_mat[h]
        n_state[h:h + 1, :] = n_vec[h]
        m_state[h:h + 1, :] = jnp.broadcast_to(m_prev[h], (1, LANES))


def _mlstm(p, b, s, off, conv_w, conv_b, b_i, b_f, blk=SEQ_BLOCK, chunk=ML_CHUNK):
    blk = min(blk, s)
    nblk = s // blk
    hq = D_QK_WIDTH // LANES
    hv = D_WIDTH // LANES
    assert off % hq == 0 and (off + 2 * hq) % hv == 0
    rowblk = lambda bi, n: bi * nblk + n
    qk_spec = lambda o: pl.BlockSpec((blk, D_QK_WIDTH), lambda bi, n: (rowblk(bi, n), off // hq + o))
    v_spec = lambda o: pl.BlockSpec((blk, D_WIDTH), lambda bi, n: (rowblk(bi, n), (off + 2 * hq) // hv + o))
    par_spec = lambda rows, o: pl.BlockSpec((rows, D_QK_WIDTH), lambda bi, n: (0, o))
    gate_off = off + 2 * hq + 2 * hv
    gbias = jnp.pad(jnp.concatenate([b_i, b_f]), (0, LANES - 2 * D_HEADS)).reshape(1, LANES)
    cb = conv_b.reshape(1, -1)
    return pl.pallas_call(
        functools.partial(_mlstm_kernel, blk=blk, chunk=chunk),
        name="mlstm",
        grid=(b, nblk),
        in_specs=[qk_spec(0), qk_spec(1), v_spec(0), v_spec(1),
                  pl.BlockSpec((blk, LANES), lambda bi, n: (rowblk(bi, n), gate_off)),
                  par_spec(D_CONV, 0), par_spec(D_CONV, 1), par_spec(1, 0), par_spec(1, 1),
                  pl.BlockSpec((1, LANES), lambda bi, n: (0, 0))],
        out_specs=pl.BlockSpec((blk, D_WIDTH), lambda bi, n: (rowblk(bi, n), 0)),
        out_shape=jax.ShapeDtypeStruct((b * s, D_WIDTH), F32),
        scratch_shapes=[pltpu.VMEM((blk + 8, D_QK_WIDTH), F32), pltpu.VMEM((blk + 8, D_QK_WIDTH), F32),
                        pltpu.VMEM((D_HEADS, D_DQK, D_DV), F32), pltpu.VMEM((8, LANES), F32),
                        pltpu.VMEM((8, LANES), F32)],
        compiler_params=_params("parallel", "arbitrary"),
    )(p, p, p, p, p, conv_w, conv_w, cb, cb, gbias)


TN = (((0,), (0,)), ((), ()))


def _chunk_masks(blk, chunk):
    t = lax.broadcasted_iota(jnp.int32, (blk, blk), 0)
    s = lax.broadcasted_iota(jnp.int32, (blk, blk), 1)
    same = (t // chunk) == (s // chunk)
    return same, same & (s <= t), same & (s < t)


def _hgrn2_kernel(q_ref, f_ref, i_ref, g_ref, lb_ref, nw_ref, o_ref, s_state, *, blk, chunk):
    @pl.when(pl.program_id(2) == 0)
    def _():
        s_state[...] = jnp.zeros_like(s_state)

    lb = lb_ref[...]
    f = f_ref[...]
    q = _silu(q_ref[...])
    v = i_ref[...]
    a = jnp.log(lb)
    bb = jnp.log1p(-lb) + _log_sigmoid(f)
    log_f = jnp.maximum(a, bb) + jnp.log1p(jnp.exp(-jnp.abs(a - bb)))
    k = (1.0 - lb) * _sigmoid(-f)

    same, tri_incl, _ = _chunk_masks(blk, chunk)
    bcum = _dot_exact_lhs(tri_incl.astype(F32), log_f)
    blast = _dot_exact_lhs(same.astype(F32), log_f)
    qe = q * jnp.exp(bcum)
    kd = k * jnp.exp(blast - bcum)
    decay = jnp.exp(blast)

    pos = lax.broadcasted_iota(jnp.int32, (blk, LANES), 0) % chunk
    o = jnp.sum(q * k, -1, keepdims=True) * v
    for dist in range(1, chunk):
        e = jnp.exp(jnp.where(pos >= dist, bcum - pltpu.roll(bcum, dist, 0), -jnp.inf))
        w = jnp.sum(q * pltpu.roll(k, dist, 0) * e, -1, keepdims=True)
        o = o + w * pltpu.roll(v, dist, 0)

    st = s_state[...]
    inter = []
    for c in range(blk // chunk):
        rows = slice(c * chunk, (c + 1) * chunk)
        inter.append(_dot(qe[rows], st, NT))
        st = st * decay[c * chunk:c * chunk + 1] + _dot(v[rows], kd[rows], TN)
    s_state[...] = st
    o = o + jnp.concatenate(inter, 0)
    o = o * lax.rsqrt(jnp.mean(o * o, -1, keepdims=True) + RMS_EPS) * nw_ref[...]
    o_ref[...] = o * _silu(g_ref[...])


def _hgrn2(p, b, s, off, lb, norm_w, blk=SEQ_BLOCK, chunk=GLA_CHUNK):
    blk = min(blk, s)
    nblk = s // blk
    spec = lambda o: pl.BlockSpec((blk, LANES), lambda bi, h, n: (bi * nblk + n, off + o * B_HEADS + h))
    return pl.pallas_call(
        functools.partial(_hgrn2_kernel, blk=blk, chunk=chunk),
        name="hgrn2",
        grid=(b, B_HEADS, nblk),
        in_specs=[spec(0), spec(1), spec(2), spec(3),
                  pl.BlockSpec((1, LANES), lambda bi, h, n: (0, h)),
                  pl.BlockSpec((1, LANES), lambda bi, h, n: (0, 0))],
        out_specs=pl.BlockSpec((blk, LANES), lambda bi, h, n: (bi * nblk + n, h)),
        out_shape=jax.ShapeDtypeStruct((b * s, B_WIDTH), F32),
        scratch_shapes=[pltpu.VMEM((B_DV, B_DK), F32)],
        compiler_params=_params("parallel", "parallel", "arbitrary"),
    )(p, p, p, p, lb.reshape(1, -1), norm_w.reshape(1, -1))


def _rwkv7_kernel(r_ref, k_ref, v_ref, wa_ref, xg_ref, mur_ref, muk_ref, muv_ref, muwa_ref, mug_ref,
                  w0_ref, a0_ref, kk_ref, ka_ref, rk_ref, lnw_ref, lnb_ref, w2_ref, a2_ref, g2_ref,
                  o_ref, tail, s_state, *, blk, chunk, npp):
    @pl.when(pl.program_id(2) == 0)
    def _():
        tail[...] = jnp.zeros_like(tail)
        s_state[...] = jnp.zeros_like(s_state)

    row0 = lax.broadcasted_iota(jnp.int32, (blk, LANES), 0) == 0

    def shift_lerp(idx, x, mu):
        prev = tail[0:1, idx * LANES:(idx + 1) * LANES]
        shifted = jnp.where(row0, prev, pltpu.roll(x, 1, 0))
        tail[0:1, idx * LANES:(idx + 1) * LANES] = x[blk - 1:blk]
        return x + (shifted - x) * mu

    wa = shift_lerp(3 * npp, wa_ref[...], muwa_ref[...])
    xg = shift_lerp(3 * npp + 1, xg_ref[...], mug_ref[...])
    tanh_wa = jnp.tanh(wa)
    sig_xg = _sigmoid(xg)

    lane = lax.broadcasted_iota(jnp.int32, (LANES, LANES), 1)
    sub = lax.broadcasted_iota(jnp.int32, (LANES, LANES), 0)
    head_bd = (lane // HEAD_DIM) == (sub // HEAD_DIM)
    head_sum = lambda x: _dot_exact_rhs(x, head_bd.astype(F32))
    same, tri_incl, _ = _chunk_masks(blk, chunk)
    same = same.astype(F32)
    tri_incl = tri_incl.astype(F32)
    c2 = 2 * chunk
    lane_c = lax.broadcasted_iota(jnp.int32, (chunk, LANES), 1)
    h1 = lane_c >= HEAD_DIM
    h0 = jnp.logical_not(h1)
    stack = lambda x: jnp.concatenate([jnp.where(h0, x, 0.0), jnp.where(h1, x, 0.0)], 0)
    ri = lax.broadcasted_iota(jnp.int32, (2 * c2, 2 * c2), 0)
    ci = lax.broadcasted_iota(jnp.int32, (2 * c2, 2 * c2), 1)
    amask = ((ci % chunk) < (ri % chunk)) | (((ci % chunk) == (ri % chunk)) & (ri >= c2))
    eye = (lax.broadcasted_iota(jnp.int32, (c2, c2), 0) == lax.broadcasted_iota(jnp.int32, (c2, c2), 1)).astype(F32)
    nchunk = blk // chunk

    pairs = []
    for pi in range(npp):
        ls = slice(pi * LANES, (pi + 1) * LANES)
        r = shift_lerp(3 * pi, r_ref[:, ls], mur_ref[:, ls])
        k = shift_lerp(3 * pi + 1, k_ref[:, ls], muk_ref[:, ls])
        v = shift_lerp(3 * pi + 2, v_ref[:, ls], muv_ref[:, ls])
        pre_w = -(w0_ref[:, ls] + _dot(tanh_wa, w2_ref[:, ls]))
        w_log = -(jnp.maximum(pre_w, 0.0) + jnp.log1p(jnp.exp(-jnp.abs(pre_w)))) - 0.5
        lw = -jnp.exp(w_log)
        a = _sigmoid(a0_ref[:, ls] + _dot(wa, a2_ref[:, ls]))
        g = _dot(sig_xg, g2_ref[:, ls])
        kk = k * kk_ref[:, ls]
        kk = kk / jnp.maximum(jnp.sqrt(head_sum(kk * kk)), 1e-12)
        k = k * (1.0 + (a - 1.0) * ka_ref[:, ls])
        bonus = head_sum(r * k * rk_ref[:, ls]) * v
        av = -kk
        bv = kk * a
        lcum = _dot_exact_lhs(tri_incl, lw)
        lsum = _dot_exact_lhs(same, lw)
        a_t = av * jnp.exp(lcum - lw)
        r_t = r * jnp.exp(lcum)
        inv = jnp.exp(-lcum)
        b_t = bv * inv
        k_t = k * inv
        to_end = jnp.exp(lsum - lcum)
        ars, amats = [], []
        for c in range(nchunk):
            rows = slice(c * chunk, (c + 1) * chunk)
            ar = jnp.concatenate([stack(a_t[rows]), stack(r_t[rows])], 0)
            bk = jnp.concatenate([stack(b_t[rows]), stack(k_t[rows])], 0)
            ars.append(ar)
            amats.append(jnp.where(amask, _dot(ar, bk, NT), 0.0))
        pairs.append(dict(v=v, g=g, bonus=bonus, b_c=bv * to_end, k_c=k * to_end, gam=jnp.exp(lsum),
                          ars=ars, amats=amats))

    keys = [(pi, c) for pi in range(npp) for c in range(nchunk)]
    xs = {key: pairs[key[0]]["amats"][key[1]][:c2, :c2] for key in keys}
    t_inv = {key: eye + xs[key] for key in keys}
    for _ in range(chunk.bit_length() - 2):
        for key in keys:
            xs[key] = _dot(xs[key], xs[key])
        for key in keys:
            t_inv[key] = t_inv[key] + _dot(t_inv[key], xs[key])

    sts = [s_state[pi] for pi in range(npp)]
    ys = [[] for _ in range(npp)]
    for c in range(nchunk):
        rows = slice(c * chunk, (c + 1) * chunk)
        for pi in range(npp):
            pr = pairs[pi]
            ar, amat = pr["ars"][c], pr["amats"][c]
            v_c = pr["v"][rows]
            v2 = jnp.concatenate([v_c, v_c], 0)
            g1 = _dot(ar, sts[pi], NT)
            u2 = _dot(t_inv[(pi, c)], g1[:c2] + _dot(amat[:c2, c2:], v2))
            y2 = g1[c2:] + _dot(amat[c2:], jnp.concatenate([u2, v2], 0))
            ys[pi].append(jnp.where(h1, y2[chunk:], y2[:chunk]))
            u = jnp.where(h1, u2[chunk:], u2[:chunk])
            upd = _dot(jnp.concatenate([u, v_c], 0), jnp.concatenate([pr["b_c"][rows], pr["k_c"][rows]], 0), TN)
            sts[pi] = jnp.where(head_bd, sts[pi] * pr["gam"][c * chunk:c * chunk + 1] + upd, 0.0)

    for pi in range(npp):
        ls = slice(pi * LANES, (pi + 1) * LANES)
        s_state[pi] = sts[pi]
        y = jnp.concatenate(ys[pi], 0)
        mu = head_sum(y) * (1.0 / HEAD_DIM)
        yc = y - mu
        var = head_sum(yc * yc) * (1.0 / HEAD_DIM)
        y = yc * lax.rsqrt(var + A_GN_EPS) * lnw_ref[:, ls] + lnb_ref[:, ls] + pairs[pi]["bonus"]
        o_ref[:, ls] = y * pairs[pi]["g"]


def _rwkv7(p, b, s, mu, w0, w2, a0, a2, g2, k_k, k_a, r_k, lnx_w, lnx_b, blk=SEQ_BLOCK, chunk=RW_CHUNK):
    blk = min(blk, s)
    nblk = s // blk
    npp = RW_PAIRS
    wide = npp * LANES
    ngrp = A_WIDTH // wide
    lora = 3 * A_WIDTH // LANES
    act = lambda o: pl.BlockSpec((blk, wide), lambda bi, hp, n: (bi * nblk + n, o * ngrp + hp))
    act_fix = lambda o: pl.BlockSpec((blk, LANES), lambda bi, hp, n: (bi * nblk + n, o))
    vec = lambda o: pl.BlockSpec((1, wide), lambda bi, hp, n: (0, o * ngrp + hp))
    vec_fix = lambda o: pl.BlockSpec((1, LANES), lambda bi, hp, n: (0, o))
    mat = pl.BlockSpec((LANES, wide), lambda bi, hp, n: (0, hp))
    row = lambda t: t.reshape(1, -1)
    w2p = jnp.concatenate([w2, jnp.zeros_like(a2)], 0)
    a2p = jnp.concatenate([jnp.zeros_like(w2), a2], 0)
    mu2 = row(mu)
    return pl.pallas_call(
        functools.partial(_rwkv7_kernel, blk=blk, chunk=chunk, npp=npp),
        name="rwkv7",
        grid=(b, ngrp, nblk),
        in_specs=[act(0), act(1), act(2), act_fix(lora), act_fix(lora + 1),
                  vec(0), vec(1), vec(2), vec_fix(lora), vec_fix(lora + 1),
                  vec(0), vec(0), vec(0), vec(0), vec(0), vec(0), vec(0), mat, mat, mat],
        out_specs=pl.BlockSpec((blk, wide), lambda bi, hp, n: (bi * nblk + n, hp)),
        out_shape=jax.ShapeDtypeStruct((b * s, A_WIDTH), F32),
        scratch_shapes=[pltpu.VMEM((8, (3 * npp + 2) * LANES), F32), pltpu.VMEM((npp, LANES, LANES), F32)],
        compiler_params=_params("parallel", "parallel", "arbitrary"),
    )(p, p, p, p, p, mu2, mu2, mu2, mu2, mu2,
      row(w0), row(a0), row(k_k), row(k_a), row(r_k), row(lnx_w), row(lnx_b), w2p, a2p, g2)


def kernel(x, ev_w_in, ev_w_out, rwkv_mu, rwkv_w0, rwkv_w2, rwkv_a0, rwkv_a2, rwkv_g2, rwkv_kk, rwkv_ka, rwkv_rk,
           rwkv_lnx_w, rwkv_lnx_b, hgrn_lb, hgrn_norm_w, od_w_in, od_w_out, mlstm_conv_w, mlstm_conv_b,
           mlstm_b_i, mlstm_b_f, ln_w, ln_b, moe_wg, moe_bg, moe_we, moe_be, moe_w1, moe_w3, moe_w2):
    b, s, d = x.shape
    depth = ln_w.shape[0]
    alpha = (2 * depth) ** 0.25
    lb_all = jnp.cumsum(jax.nn.softmax(hgrn_lb.astype(F32), axis=0), axis=0)
    lb_all = lb_all - lb_all[:1]
    tn = 2 * LANES
    xf = x.reshape(b * s, d)
    for layer in range(depth):
        j = layer // 2
        if layer % 2 == 0:
            p = _matmul(xf, ev_w_in[j].astype(BF16), tn=tn)
            ya = _rwkv7(p, b, s, rwkv_mu[j], rwkv_w0[j], rwkv_w2[j], rwkv_a0[j], rwkv_a2[j], rwkv_g2[j],
                        rwkv_kk[j], rwkv_ka[j], rwkv_rk[j], rwkv_lnx_w[j], rwkv_lnx_b[j])
            yb = _hgrn2(p, b, s, A_COLS // LANES, lb_all[j], hgrn_norm_w[j])
            w_out = ev_w_out[j]
        else:
            w_in = od_w_in[j]
            w_in = jnp.pad(w_in, ((0, 0), (0, -w_in.shape[1] % tn))).astype(BF16)
            p = _matmul(xf, w_in, tn=tn)
            nq = C_WIDTH // LANES
            ya = _dilated_attention(p, b, s, 0, nq, 2 * nq)
            yb = _mlstm(p, b, s, 3 * nq, mlstm_conv_w[j], mlstm_conv_b[j], mlstm_b_i[j], mlstm_b_f[j])
            w_out = od_w_out[j]
        xf = _proj_ln(ya, yb, w_out, xf, ln_w[layer, 0], ln_b[layer, 0], alpha)
        xf = _moe_sublayer(xf, moe_wg[layer], moe_bg[layer], moe_we[layer], moe_be[layer],
                           moe_w1[layer], moe_w3[layer], moe_w2[layer], ln_w[layer, 1], ln_b[layer, 1], alpha)
    return xf.reshape(b, s, d)
```

```python
import functools

import jax
import jax.numpy as jnp
from jax import lax
from jax.experimental import pallas as pl
from jax.experimental.pallas import tpu as pltpu

F32 = jnp.float32
BF16 = jnp.bfloat16

HEAD_DIM = 64
A_HEADS = 16
A_WIDTH = A_HEADS * HEAD_DIM
A_LORA_W = 64
A_LORA_A = 64
A_LORA_G = 128
A_COLS = 3 * A_WIDTH + A_LORA_W + A_LORA_A + A_LORA_G
A_GN_EPS = 64e-5
B_HEADS = 8
B_DK = 128
B_DV = 128
B_WIDTH = B_HEADS * B_DV
C_HEADS = 16
C_WIDTH = C_HEADS * HEAD_DIM
C_PAIRS = ((128, 1), (512, 4), (2048, 16))
D_HEADS = 4
D_DQK = 128
D_DV = 256
D_QK_WIDTH = D_HEADS * D_DQK
D_WIDTH = D_HEADS * D_DV
D_CONV = 4
ODD_COLS = 3 * C_WIDTH + 2 * D_QK_WIDTH + 2 * D_WIDTH + 2 * D_HEADS
N_GROUPS = 4
EXPERTS_PER_GROUP = 8
N_EXPERTS = N_GROUPS * EXPERTS_PER_GROUP
LN_EPS = 1e-5
RMS_EPS = 1e-6

LANES = 128
VMEM_LIMIT = 56 * 1024 * 1024
RW_CHUNK = 64
RW_PAIRS = 4
GLA_WIDE_CHUNK = 64
GLA_SAFE_LOG = 60.0
GLA_CHUNK = 16
ML_CHUNK = 64
SEQ_BLOCK = 256
ATT_BLOCK = 128
ATT_GROUP = 8
MOE_BM = 256
TOK_BITS = 15

NN = (((1,), (0,)), ((), ()))
NT = (((1,), (1,)), ((), ()))


def _dot(a, b, dims=NN):
    return lax.dot_general(a.astype(BF16), b.astype(BF16), dims, preferred_element_type=F32)


def _split3(x):
    hi = x.astype(BF16)
    r1 = x - hi.astype(F32)
    mid = r1.astype(BF16)
    lo = (r1 - mid.astype(F32)).astype(BF16)
    return hi, mid, lo


def _dot_exact_lhs(mask, x, dims=NN):
    hi, mid, lo = _split3(x)
    m = mask.astype(BF16)
    d = lambda p: lax.dot_general(m, p, dims, preferred_element_type=F32)
    return d(hi) + d(mid) + d(lo)


def _dot_exact_rhs(x, mask):
    hi, mid, lo = _split3(x)
    m = mask.astype(BF16)
    d = lambda p: lax.dot_general(p, m, NN, preferred_element_type=F32)
    return d(hi) + d(mid) + d(lo)


def _sigmoid(x):
    return 1.0 / (1.0 + jnp.exp(-x))


def _silu(x):
    return x * _sigmoid(x)


def _log_sigmoid(x):
    return jnp.minimum(x, 0.0) - jnp.log1p(jnp.exp(-jnp.abs(x)))


def _params(*sem):
    return pltpu.CompilerParams(dimension_semantics=sem, vmem_limit_bytes=VMEM_LIMIT)


def _mm_kernel(x_ref, w_ref, o_ref):
    o_ref[...] = _dot(x_ref[...], w_ref[...])


def _matmul(x, w, tm=2048, tn=256):
    t, k = x.shape
    n = w.shape[1]
    tm = min(tm, t)
    return pl.pallas_call(
        _mm_kernel,
        name="in_proj",
        grid=(t // tm, n // tn),
        in_specs=[pl.BlockSpec((tm, k), lambda i, j: (i, 0)),
                  pl.BlockSpec((k, tn), lambda i, j: (0, j))],
        out_specs=pl.BlockSpec((tm, tn), lambda i, j: (i, j)),
        out_shape=jax.ShapeDtypeStruct((t, n), F32),
        compiler_params=_params("parallel", "arbitrary"),
    )(x, w)


def _layer_norm(y, w, b):
    mu = jnp.mean(y, -1, keepdims=True)
    yc = y - mu
    var = jnp.mean(yc * yc, -1, keepdims=True)
    return yc * lax.rsqrt(var + LN_EPS) * w + b


def _proj_ln_kernel(ya_ref, yb_ref, wa_ref, wb_ref, x_ref, g_ref, b_ref, o_ref, *, alpha):
    y = _dot(ya_ref[...], wa_ref[...]) + _dot(yb_ref[...], wb_ref[...]) + alpha * x_ref[...]
    o_ref[...] = _layer_norm(y, g_ref[...], b_ref[...])


def _proj_ln(ya, yb, w_out, x, g, b, alpha, tm=256):
    t, d = x.shape
    ka = ya.shape[1]
    kb = yb.shape[1]
    tm = min(tm, t)
    wa = w_out[:ka].astype(BF16)
    wb = w_out[ka:].astype(BF16)
    row = lambda i: (i, 0)
    fix = lambda i: (0, 0)
    return pl.pallas_call(
        functools.partial(_proj_ln_kernel, alpha=alpha),
        name="proj_ln",
        grid=(t // tm,),
        in_specs=[pl.BlockSpec((tm, ka), row), pl.BlockSpec((tm, kb), row),
                  pl.BlockSpec((ka, d), fix), pl.BlockSpec((kb, d), fix),
                  pl.BlockSpec((tm, d), row), pl.BlockSpec((1, d), fix), pl.BlockSpec((1, d), fix)],
        out_specs=pl.BlockSpec((tm, d), row),
        out_shape=jax.ShapeDtypeStruct((t, d), F32),
        compiler_params=_params("parallel"),
    )(ya, yb, wa, wb, x, g.reshape(1, d), b.reshape(1, d))


def _router_kernel(x_ref, wh_ref, wm_ref, b_ref, o_ref):
    xh, xm, _ = _split3(x_ref[...])
    wh = wh_ref[...]
    d = lambda a, w: lax.dot_general(a, w, NN, preferred_element_type=F32)
    logits = d(xh, wh) + d(xh, wm_ref[...]) + d(xm, wh) + b_ref[...]
    tm = logits.shape[0]
    lane = lax.broadcasted_iota(jnp.int32, (tm, LANES), 1)
    neg = -jnp.inf
    gl = jnp.where(lane < N_GROUPS, logits, neg)
    gmax = jnp.max(gl, -1, keepdims=True)
    g_top = jnp.min(jnp.where(gl == gmax, lane, LANES), -1, keepdims=True)
    p_group = 1.0 / jnp.sum(jnp.exp(gl - gmax), -1, keepdims=True)
    e_lane = lane - N_GROUPS
    in_group = (e_lane >= 0) & (e_lane < N_EXPERTS) & ((e_lane // EXPERTS_PER_GROUP) == g_top)
    el = jnp.where(in_group, logits, neg)
    v1 = jnp.max(el, -1, keepdims=True)
    i1 = jnp.min(jnp.where(el == v1, lane, LANES), -1, keepdims=True)
    el2 = jnp.where(lane == i1, neg, el)
    v2 = jnp.max(el2, -1, keepdims=True)
    i2 = jnp.min(jnp.where(el2 == v2, lane, LANES), -1, keepdims=True)
    e2 = jnp.exp(v2 - v1)
    w1 = 1.0 / (1.0 + e2)
    w2 = e2 / (1.0 + e2)
    out = jnp.where(lane == 0, (i1 - N_GROUPS).astype(F32),
                    jnp.where(lane == 1, (i2 - N_GROUPS).astype(F32),
                              jnp.where(lane == 2, p_group * w1,
                                        jnp.where(lane == 3, p_group * w2, 0.0))))
    o_ref[...] = out


def _router(x, wg, bg, we, be, tm=512):
    t, d = x.shape
    tm = min(tm, t)
    w = jnp.concatenate([wg, jnp.transpose(we, (1, 0, 2)).reshape(d, N_EXPERTS)], 1)
    w = jnp.pad(w, ((0, 0), (0, LANES - w.shape[1])))
    bias = jnp.pad(jnp.concatenate([bg, be.reshape(-1)]), (0, LANES - N_GROUPS - N_EXPERTS)).reshape(1, LANES)
    wh = w.astype(BF16)
    wm = (w - wh.astype(F32)).astype(BF16)
    fix = lambda i: (0, 0)
    return pl.pallas_call(
        _router_kernel,
        name="router",
        grid=(t // tm,),
        in_specs=[pl.BlockSpec((tm, d), lambda i: (i, 0)), pl.BlockSpec((d, LANES), fix),
                  pl.BlockSpec((d, LANES), fix), pl.BlockSpec((1, LANES), fix)],
        out_specs=pl.BlockSpec((tm, LANES), lambda i: (i, 0)),
        out_shape=jax.ShapeDtypeStruct((t, LANES), F32),
        compiler_params=_params("parallel"),
    )(x, wh, wm, bias)


def _ffn_kernel(bexp_ref, nvalid_ref, pk_ref, x_hbm, w1_ref, w3_ref, w2_ref, y_hbm,
                xbuf, ybuf, w1b, w3b, w2b, gsem, ssem, *, bm):
    j = pl.program_id(0)
    nb = pl.num_programs(0)
    nvalid = nvalid_ref[0]
    slot = j % 2

    @pl.when((j < nvalid) & ((j == 0) | (bexp_ref[j] != bexp_ref[jnp.maximum(j - 1, 0)])))
    def _():
        w1b[...] = w1_ref[...].astype(BF16)
        w3b[...] = w3_ref[...].astype(BF16)
        w2b[...] = w2_ref[...].astype(BF16)

    def row_copy_in(blk, s, r):
        src = pk_ref[blk * bm + r] & ((1 << TOK_BITS) - 1)
        return pltpu.make_async_copy(x_hbm.at[pl.ds(src, 1)], xbuf.at[s, pl.ds(r, 1)], gsem.at[s])

    def row_copy_out(blk, s, r):
        dst = lax.shift_right_logical(pk_ref[blk * bm + r], TOK_BITS)
        return pltpu.make_async_copy(ybuf.at[s, pl.ds(r, 1)], y_hbm.at[pl.ds(dst, 1)], ssem.at[s])

    def for_rows(fn):
        def body(r, c):
            fn(r)
            return c
        lax.fori_loop(0, bm, body, 0, unroll=8)

    @pl.when((j == 0) & (nvalid > 0))
    def _():
        for_rows(lambda r: row_copy_in(0, 0, r).start())

    @pl.when(j + 1 < nvalid)
    def _():
        for_rows(lambda r: row_copy_in(j + 1, 1 - slot, r).start())

    def wait_gather(s):
        pltpu.make_async_copy(xbuf.at[s], xbuf.at[s], gsem.at[s]).wait()

    def wait_scatter(s):
        pltpu.make_async_copy(ybuf.at[s], ybuf.at[s], ssem.at[s]).wait()

    @pl.when((j >= 2) & (j - 2 < nvalid))
    def _():
        wait_scatter(slot)

    @pl.when(j < nvalid)
    def _():
        wait_gather(slot)
        xb = xbuf[slot]
        h = _silu(_dot(xb, w1b[...])) * _dot(xb, w3b[...])
        ybuf[slot] = _dot(h, w2b[...])
        for_rows(lambda r: row_copy_out(j, slot, r).start())

    @pl.when(j >= nvalid)
    def _():
        ybuf[slot] = jnp.zeros((bm, ybuf.shape[2]), F32)
        fill = pltpu.make_async_copy(ybuf.at[slot], y_hbm.at[pl.ds(j * bm, bm)], ssem.at[slot])
        fill.start()
        fill.wait()

    @pl.when(j == nb - 1)
    def _():
        @pl.when((j >= 1) & (j - 1 < nvalid))
        def _():
            wait_scatter(1 - slot)

        @pl.when(j < nvalid)
        def _():
            wait_scatter(slot)


def _moe_dispatch(slab, t, bm):
    n = 2 * t
    e = N_EXPERTS
    rows = -(-(n + e * (bm - 1)) // bm) * bm
    ids = slab[:, :2].astype(jnp.int32).reshape(n)
    order = jnp.argsort(ids, stable=True).astype(jnp.int32)
    counts = jnp.sum((ids[:, None] == jnp.arange(e, dtype=jnp.int32)[None, :]).astype(jnp.int32), 0)
    padded = -(-counts // bm) * bm
    pend = jnp.cumsum(padded)
    pstart = pend - padded
    start = jnp.cumsum(counts) - counts
    n_blocks = rows // bm
    bstart = jnp.arange(n_blocks, dtype=jnp.int32) * bm
    block_exp = jnp.minimum(jnp.sum((bstart[:, None] >= pend[None, :]).astype(jnp.int32), 1), e - 1)
    nvalid = (pend[-1] // bm).astype(jnp.int32).reshape(1)
    per_row = lambda tbl: jnp.repeat(tbl[block_exp], bm)
    local = jnp.arange(rows, dtype=jnp.int32) - per_row(pstart)
    valid = local < per_row(counts)
    a = order[jnp.clip(per_row(start) + local, 0, n - 1)]
    pad_rank = jnp.cumsum(1 - valid.astype(jnp.int32)) - 1
    row_src = jnp.where(valid, a // 2, 0)
    row_dst = jnp.where(valid, (a % 2) * t + a // 2, n + pad_rank)
    packed = row_src | (row_dst << TOK_BITS)
    return packed.astype(jnp.int32), block_exp, nvalid, rows


def _moe_ffn(x, slab, w1, w3, w2, layer, bm=MOE_BM):
    t, d = x.shape
    de = w1.shape[3]
    assert t <= (1 << TOK_BITS)
    packed, block_exp, nvalid, rows = _moe_dispatch(slab, t, bm)
    assert rows < (1 << (32 - TOK_BITS))
    n_blocks = rows // bm
    wmap = lambda j, bexp, nv, pk: (layer, bexp[j], 0, 0)
    grid_spec = pltpu.PrefetchScalarGridSpec(
        num_scalar_prefetch=3,
        grid=(n_blocks,),
        in_specs=[pl.BlockSpec(memory_space=pl.ANY),
                  pl.BlockSpec((None, None, d, de), wmap), pl.BlockSpec((None, None, d, de), wmap),
                  pl.BlockSpec((None, None, de, d), wmap)],
        out_specs=pl.BlockSpec(memory_space=pl.ANY),
        scratch_shapes=[pltpu.VMEM((2, bm, d), F32), pltpu.VMEM((2, bm, d), F32),
                        pltpu.VMEM((d, de), BF16), pltpu.VMEM((d, de), BF16), pltpu.VMEM((de, d), BF16),
                        pltpu.SemaphoreType.DMA((2,)), pltpu.SemaphoreType.DMA((2,))],
    )
    return pl.pallas_call(
        functools.partial(_ffn_kernel, bm=bm),
        name="moe_ffn",
        grid_spec=grid_spec,
        out_shape=jax.ShapeDtypeStruct((rows, d), F32),
        compiler_params=_params("arbitrary"),
    )(block_exp, nvalid, packed, x, w1, w3, w2)


def _moe_ln_kernel(x_ref, y0_ref, y1_ref, s_ref, g_ref, b_ref, o_ref, *, alpha):
    s = s_ref[...]
    y = alpha * x_ref[...] + s[:, 2:3] * y0_ref[...] + s[:, 3:4] * y1_ref[...]
    o_ref[...] = _layer_norm(y, g_ref[...], b_ref[...])


def _moe_ln(x, y, slab, g, b, alpha, tm=512):
    t, d = x.shape
    tm = min(tm, t)
    nt = t // tm
    row = lambda i: (i, 0)
    fix = lambda i: (0, 0)
    return pl.pallas_call(
        functools.partial(_moe_ln_kernel, alpha=alpha),
        name="moe_ln",
        grid=(nt,),
        in_specs=[pl.BlockSpec((tm, d), row), pl.BlockSpec((tm, d), row),
                  pl.BlockSpec((tm, d), lambda i: (i + nt, 0)), pl.BlockSpec((tm, LANES), row),
                  pl.BlockSpec((1, d), fix), pl.BlockSpec((1, d), fix)],
        out_specs=pl.BlockSpec((tm, d), row),
        out_shape=jax.ShapeDtypeStruct((t, d), F32),
        compiler_params=_params("parallel"),
    )(x, y, y, slab, g.reshape(1, d), b.reshape(1, d))


def _moe_sublayer(x, wg, bg, we, be, w1, w3, w2, layer, g, b, alpha):
    slab = _router(x, wg, bg, we, be)
    y = _moe_ffn(x, slab, w1, w3, w2, layer)
    return _moe_ln(x, y, slab, g, b, alpha)


def _attn_kernel(q_ref, kp_ref, kc_ref, vp_ref, vc_ref, o_ref, k_all, v_all, q_all, acc_o, acc_m, acc_l, *, rows):
    n = pl.program_id(2)
    k_all[0:rows, :] = kp_ref[...]
    k_all[rows:2 * rows, :] = kc_ref[...]
    v_all[0:rows, :] = vp_ref[...]
    v_all[rows:2 * rows, :] = vc_ref[...]
    q_all[...] = q_ref[...] * (HEAD_DIM ** -0.5)
    nk = ATT_BLOCK
    qi = lax.broadcasted_iota(jnp.int32, (nk, 2 * nk), 0)
    kj = lax.broadcasted_iota(jnp.int32, (nk, 2 * nk), 1)
    band = (kj >= qi) & (kj <= qi + nk)
    lane = lax.broadcasted_iota(jnp.int32, (nk, LANES), 1)
    head1 = lane >= HEAD_DIM

    for p_idx, (window, dil) in enumerate(C_PAIRS):
        assert window // dil == nk
        span = nk * dil
        n_sub = rows // span

        def group(gi, carry, dil=dil, span=span, n_sub=n_sub, first=(p_idx == 0)):
            sel = lambda ref, start: ref[pl.ds(start, nk, stride=dil), :]
            loaded = []
            for i in range(ATT_GROUP):
                u = gi * ATT_GROUP + i
                res = u // n_sub
                sub = u - res * n_sub
                qstart = sub * span + res
                q = sel(q_all, qstart)
                k2 = jnp.concatenate([sel(k_all, rows + qstart - span), sel(k_all, rows + qstart)], 0).astype(BF16)
                v2 = jnp.concatenate([sel(v_all, rows + qstart - span), sel(v_all, rows + qstart)], 0).astype(BF16)
                dst = pl.ds(qstart, nk, stride=dil)
                old = None if first else (acc_o[dst, :], acc_m[dst, :], acc_l[dst, :])
                loaded.append((q, k2, v2, dst, old, sub))
            results = []
            for q, k2, v2, dst, old, sub in loaded:
                ok = band & ((kj >= nk) | (n > 0) | (sub > 0))
                outs = []
                for h1 in (False, True):
                    hm = head1 if h1 else jnp.logical_not(head1)
                    sc = _dot(jnp.where(hm, q, 0.0), k2, NT)
                    sc = jnp.where(ok, sc, -jnp.inf)
                    m = jnp.max(sc, -1, keepdims=True)
                    p = jnp.exp(sc - m)
                    outs.append((_dot(p, v2), m, jnp.sum(p, -1, keepdims=True)))
                o = jnp.where(head1, outs[1][0], outs[0][0])
                m = jnp.where(head1, outs[1][1], outs[0][1])
                l = jnp.where(head1, outs[1][2], outs[0][2])
                if old is not None:
                    o_old, m_old, l_old = old
                    m_new = jnp.maximum(m_old, m)
                    a_old = jnp.exp(m_old - m_new)
                    a_cur = jnp.exp(m - m_new)
                    o = o_old * a_old + o * a_cur
                    l = l_old * a_old + l * a_cur
                    m = m_new
                results.append((dst, o, m, l))
            for dst, o, m, l in results:
                acc_o[dst, :] = o
                acc_m[dst, :] = m
                acc_l[dst, :] = l
            return carry

        assert (dil * n_sub) % ATT_GROUP == 0
        lax.fori_loop(0, dil * n_sub // ATT_GROUP, group, 0)

    o_ref[...] = acc_o[...] / acc_l[...]


def _dilated_attention(p, b, s, qoff, koff, voff):
    rows = max(w for w, _ in C_PAIRS)
    assert s % rows == 0
    nblk = s // rows
    cur = lambda off: pl.BlockSpec((rows, LANES), lambda bi, hp, n: (bi * nblk + n, off + hp))
    prev = lambda off: pl.BlockSpec((rows, LANES), lambda bi, hp, n: (bi * nblk + jnp.maximum(n - 1, 0), off + hp))
    scratch = [pltpu.VMEM((2 * rows, LANES), F32), pltpu.VMEM((2 * rows, LANES), F32)] + \
              [pltpu.VMEM((rows, LANES), F32)] * 4
    return pl.pallas_call(
        functools.partial(_attn_kernel, rows=rows),
        name="dil_attn",
        grid=(b, C_WIDTH // LANES, nblk),
        in_specs=[cur(qoff), prev(koff), cur(koff), prev(voff), cur(voff)],
        out_specs=pl.BlockSpec((rows, LANES), lambda bi, hp, n: (bi * nblk + n, hp)),
        out_shape=jax.ShapeDtypeStruct((b * s, C_WIDTH), F32),
        scratch_shapes=scratch,
        compiler_params=_params("parallel", "parallel", "arbitrary"),
    )(p, p, p, p, p)


def _causal_conv(x_ref, pad_ref, w_ref, b_ref, first, blk):
    @pl.when(first)
    def _():
        pad_ref[0:8, :] = jnp.zeros((8, pad_ref.shape[1]), F32)
    pad_ref[8:8 + blk, :] = x_ref[...]
    w = w_ref[...]
    acc = b_ref[...]
    for j in range(D_CONV):
        acc = acc + w[j:j + 1, :] * pad_ref[pl.ds(8 - (D_CONV - 1) + j, blk), :]
    pad_ref[0:8, :] = pad_ref[blk:blk + 8, :]
    return acc


def _mlstm_kernel(q_ref, k_ref, v_ref, og_ref, gate_ref, cwq_ref, cwk_ref, cbq_ref, cbk_ref, gb_ref,
                  o_ref, qpad, kpad, c_state, n_state, m_state, *, blk, chunk):
    first = pl.program_id(1) == 0

    @pl.when(first)
    def _():
        c_state[...] = jnp.zeros_like(c_state)
        n_state[...] = jnp.zeros_like(n_state)
        m_state[...] = jnp.zeros_like(m_state)

    qc = _silu(_causal_conv(q_ref, qpad, cwq_ref, cbq_ref, first, blk))
    kc = _silu(_causal_conv(k_ref, kpad, cwk_ref, cbk_ref, first, blk)) * (D_DQK ** -0.5)
    gates = gate_ref[...] + gb_ref[...]
    logf = _log_sigmoid(gates)
    lane = lax.broadcasted_iota(jnp.int32, (chunk, LANES), 1)
    ti = lax.broadcasted_iota(jnp.int32, (chunk, chunk), 0)
    si = lax.broadcasted_iota(jnp.int32, (chunk, chunk), 1)
    causal = si <= ti
    tri = causal.astype(F32)

    c_mat = [c_state[h] for h in range(D_HEADS)]
    n_vec = [n_state[h:h + 1, :] for h in range(D_HEADS)]
    m_prev = [m_state[h:h + 1, 0:1] for h in range(D_HEADS)]
    for c in range(blk // chunk):
        rows = slice(c * chunk, (c + 1) * chunk)
        gi = gates[rows]
        fcum_all = _dot_exact_lhs(tri, logf[rows])
        for h in range(D_HEADS):
            sel_i = lane == h
            sel_f = lane == D_HEADS + h
            qs = slice(h * D_DQK, (h + 1) * D_DQK)
            vs = slice(h * D_DV, (h + 1) * D_DV)
            q_c, k_c, v_c = qc[rows, qs], kc[rows, qs], v_ref[rows, vs]
            f_col = jnp.sum(jnp.where(sel_f, fcum_all, 0.0), -1, keepdims=True)
            i_col = jnp.sum(jnp.where(sel_i, gi, 0.0), -1, keepdims=True)
            f_row = _dot_exact_lhs(sel_f.astype(F32), fcum_all, NT)
            i_row = _dot_exact_lhs(sel_i.astype(F32), gi, NT)
            log_d = jnp.where(causal, f_col - f_row + i_row, -jnp.inf)
            log_inter = f_col + m_prev[h]
            m_t = jnp.maximum(jnp.max(log_d, -1, keepdims=True), log_inter)
            w_inter = jnp.exp(log_inter - m_t)
            scores = _dot(q_c, k_c, NT) * jnp.exp(log_d - m_t)
            num = w_inter * _dot(q_c, c_mat[h]) + _dot(scores, v_c)
            den = w_inter * jnp.sum(q_c * n_vec[h], -1, keepdims=True) + jnp.sum(scores, -1, keepdims=True)
            out = num / jnp.maximum(jnp.abs(den), jnp.exp(-m_t))
            o_ref[rows, vs] = _sigmoid(og_ref[rows, vs]) * out
            f_tot = f_col[chunk - 1:chunk]
            log_w = f_tot - f_col + i_col
            m_new = jnp.maximum(f_tot + m_prev[h], jnp.max(log_w, 0, keepdims=True))
            kw = k_c * jnp.exp(log_w - m_new)
            w_old = jnp.exp(f_tot + m_prev[h] - m_new)
            c_mat[h] = w_old * c_mat[h] + _dot(kw.T, v_c)
            n_vec[h] = w_old * n_vec[h] + jnp.sum(kw, 0, keepdims=True)
            m_prev[h] = m_new
    for h in range(D_HEADS):
        c_state[h] = c_mat[h]
        n_state[h:h + 1, :] = n_vec[h]
        m_state[h:h + 1, :] = jnp.broadcast_to(m_prev[h], (1, LANES))


def _mlstm(p, b, s, off, conv_w, conv_b, b_i, b_f, blk=SEQ_BLOCK, chunk=ML_CHUNK):
    blk = min(blk, s)
    nblk = s // blk
    hq = D_QK_WIDTH // LANES
    hv = D_WIDTH // LANES
    assert off % hq == 0 and (off + 2 * hq) % hv == 0
    rowblk = lambda bi, n: bi * nblk + n
    qk_spec = lambda o: pl.BlockSpec((blk, D_QK_WIDTH), lambda bi, n: (rowblk(bi, n), off // hq + o))
    v_spec = lambda o: pl.BlockSpec((blk, D_WIDTH), lambda bi, n: (rowblk(bi, n), (off + 2 * hq) // hv + o))
    par_spec = lambda rows, o: pl.BlockSpec((rows, D_QK_WIDTH), lambda bi, n: (0, o))
    gate_off = off + 2 * hq + 2 * hv
    gbias = jnp.pad(jnp.concatenate([b_i, b_f]), (0, LANES - 2 * D_HEADS)).reshape(1, LANES)
    cb = conv_b.reshape(1, -1)
    return pl.pallas_call(
        functools.partial(_mlstm_kernel, blk=blk, chunk=chunk),
        name="mlstm",
        grid=(b, nblk),
        in_specs=[qk_spec(0), qk_spec(1), v_spec(0), v_spec(1),
                  pl.BlockSpec((blk, LANES), lambda bi, n: (rowblk(bi, n), gate_off)),
                  par_spec(D_CONV, 0), par_spec(D_CONV, 1), par_spec(1, 0), par_spec(1, 1),
                  pl.BlockSpec((1, LANES), lambda bi, n: (0, 0))],
        out_specs=pl.BlockSpec((blk, D_WIDTH), lambda bi, n: (rowblk(bi, n), 0)),
        out_shape=jax.ShapeDtypeStruct((b * s, D_WIDTH), F32),
        scratch_shapes=[pltpu.VMEM((blk + 8, D_QK_WIDTH), F32), pltpu.VMEM((blk + 8, D_QK_WIDTH), F32),
                        pltpu.VMEM((D_HEADS, D_DQK, D_DV), F32), pltpu.VMEM((8, LANES), F32),
                        pltpu.VMEM((8, LANES), F32)],
        compiler_params=_params("parallel", "arbitrary"),
    )(p, p, p, p, p, conv_w, conv_w, cb, cb, gbias)


TN = (((0,), (0,)), ((), ()))


def _chunk_masks(blk, chunk):
    t = lax.broadcasted_iota(jnp.int32, (blk, blk), 0)
    s = lax.broadcasted_iota(jnp.int32, (blk, blk), 1)
    same = (t // chunk) == (s // chunk)
    return same, same & (s <= t), same & (s < t)


def _hgrn2_kernel(q_ref, f_ref, i_ref, g_ref, lb_ref, nw_ref, o_ref, s_state, o_scr, *, blk, chunk, wide):
    @pl.when(pl.program_id(2) == 0)
    def _():
        s_state[...] = jnp.zeros_like(s_state)

    lb = lb_ref[...]
    f = f_ref[...]
    q = _silu(q_ref[...])
    v = i_ref[...]
    a = jnp.log(lb)
    bb = jnp.log1p(-lb) + _log_sigmoid(f)
    log_f = jnp.maximum(a, bb) + jnp.log1p(jnp.exp(-jnp.abs(a - bb)))
    k = (1.0 - lb) * _sigmoid(-f)

    _, tri_w, _ = _chunk_masks(blk, wide)
    bw = _dot_exact_lhs(tri_w.astype(F32), log_f)
    wchunks = [bw[c * wide:(c + 1) * wide] for c in range(blk // wide)]
    mids = [bc[wide // 2 - 1:wide // 2] for bc in wchunks]
    spread = [jnp.max(jnp.abs(bc - m)) for bc, m in zip(wchunks, mids)]
    safe = functools.reduce(jnp.maximum, spread) < GLA_SAFE_LOG

    @pl.when(safe)
    def _():
        ti = lax.broadcasted_iota(jnp.int32, (wide, wide), 0)
        si = lax.broadcasted_iota(jnp.int32, (wide, wide), 1)
        st = s_state[...]
        for c, (bc, m) in enumerate(zip(wchunks, mids)):
            rows = slice(c * wide, (c + 1) * wide)
            q_c, k_c, v_c = q[rows], k[rows], v[rows]
            total = bc[wide - 1:wide]
            att = jnp.where(si <= ti, _dot(q_c * jnp.exp(bc - m), k_c * jnp.exp(m - bc), NT), 0.0)
            o_scr[rows, :] = _dot(att, v_c) + _dot(q_c * jnp.exp(bc), st, NT)
            st = st * jnp.exp(total) + _dot(v_c, k_c * jnp.exp(total - bc), TN)
        s_state[...] = st

    @pl.when(jnp.logical_not(safe))
    def _():
        same, tri_incl, _ = _chunk_masks(blk, chunk)
        bcum = _dot_exact_lhs(tri_incl.astype(F32), log_f)
        blast = _dot_exact_lhs(same.astype(F32), log_f)
        qe = q * jnp.exp(bcum)
        kd = k * jnp.exp(blast - bcum)
        decay = jnp.exp(blast)
        pos = lax.broadcasted_iota(jnp.int32, (chunk, LANES), 0)
        st = s_state[...]
        for c in range(blk // chunk):
            rows = slice(c * chunk, (c + 1) * chunk)
            q_c, k_c, v_c, b_c = q[rows], k[rows], v[rows], bcum[rows]
            o_c = _dot(qe[rows], st, NT) + jnp.sum(q_c * k_c, -1, keepdims=True) * v_c
            for dist in range(1, chunk):
                e = jnp.exp(jnp.where(pos >= dist, b_c - pltpu.roll(b_c, dist, 0), -jnp.inf))
                w = jnp.sum(q_c * pltpu.roll(k_c, dist, 0) * e, -1, keepdims=True)
                o_c = o_c + w * pltpu.roll(v_c, dist, 0)
            o_scr[rows, :] = o_c
            st = st * decay[c * chunk:c * chunk + 1] + _dot(v_c, kd[rows], TN)
        s_state[...] = st

    o = o_scr[...]
    o = o * lax.rsqrt(jnp.mean(o * o, -1, keepdims=True) + RMS_EPS) * nw_ref[...]
    o_ref[...] = o * _silu(g_ref[...])


def _hgrn2(p, b, s, off, lb, norm_w, blk=SEQ_BLOCK, chunk=GLA_CHUNK):
    blk = min(blk, s)
    nblk = s // blk
    spec = lambda o: pl.BlockSpec((blk, LANES), lambda bi, h, n: (bi * nblk + n, off + o * B_HEADS + h))
    return pl.pallas_call(
        functools.partial(_hgrn2_kernel, blk=blk, chunk=chunk, wide=min(GLA_WIDE_CHUNK, blk)),
        name="hgrn2",
        grid=(b, B_HEADS, nblk),
        in_specs=[spec(0), spec(1), spec(2), spec(3),
                  pl.BlockSpec((1, LANES), lambda bi, h, n: (0, h)),
                  pl.BlockSpec((1, LANES), lambda bi, h, n: (0, 0))],
        out_specs=pl.BlockSpec((blk, LANES), lambda bi, h, n: (bi * nblk + n, h)),
        out_shape=jax.ShapeDtypeStruct((b * s, B_WIDTH), F32),
        scratch_shapes=[pltpu.VMEM((B_DV, B_DK), F32), pltpu.VMEM((blk, LANES), F32)],
        compiler_params=_params("parallel", "parallel", "arbitrary"),
    )(p, p, p, p, lb.reshape(1, -1), norm_w.reshape(1, -1))


def _rwkv7_kernel(r_ref, k_ref, v_ref, wa_ref, xg_ref, mur_ref, muk_ref, muv_ref, muwa_ref, mug_ref,
                  w0_ref, a0_ref, kk_ref, ka_ref, rk_ref, lnw_ref, lnb_ref, w2_ref, a2_ref, g2_ref,
                  o_ref, tail, s_state, *, blk, chunk, npp):
    @pl.when(pl.program_id(2) == 0)
    def _():
        tail[...] = jnp.zeros_like(tail)
        s_state[...] = jnp.zeros_like(s_state)

    row0 = lax.broadcasted_iota(jnp.int32, (blk, LANES), 0) == 0

    def shift_lerp(idx, x, mu):
        prev = tail[0:1, idx * LANES:(idx + 1) * LANES]
        shifted = jnp.where(row0, prev, pltpu.roll(x, 1, 0))
        tail[0:1, idx * LANES:(idx + 1) * LANES] = x[blk - 1:blk]
        return x + (shifted - x) * mu

    wa = shift_lerp(3 * npp, wa_ref[...], muwa_ref[...])
    xg = shift_lerp(3 * npp + 1, xg_ref[...], mug_ref[...])
    tanh_wa = jnp.tanh(wa)
    sig_xg = _sigmoid(xg)

    lane = lax.broadcasted_iota(jnp.int32, (LANES, LANES), 1)
    sub = lax.broadcasted_iota(jnp.int32, (LANES, LANES), 0)
    head_bd = (lane // HEAD_DIM) == (sub // HEAD_DIM)
    head_sum = lambda x: _dot_exact_rhs(x, head_bd.astype(F32))
    same, tri_incl, _ = _chunk_masks(blk, chunk)
    same = same.astype(F32)
    tri_incl = tri_incl.astype(F32)
    c2 = 2 * chunk
    lane_c = lax.broadcasted_iota(jnp.int32, (chunk, LANES), 1)
    h1 = lane_c >= HEAD_DIM
    h0 = jnp.logical_not(h1)
    stack = lambda x: jnp.concatenate([jnp.where(h0, x, 0.0), jnp.where(h1, x, 0.0)], 0)
    ri = lax.broadcasted_iota(jnp.int32, (2 * c2, 2 * c2), 0)
    ci = lax.broadcasted_iota(jnp.int32, (2 * c2, 2 * c2), 1)
    amask = ((ci % chunk) < (ri % chunk)) | (((ci % chunk) == (ri % chunk)) & (ri >= c2))
    eye = (lax.broadcasted_iota(jnp.int32, (c2, c2), 0) == lax.broadcasted_iota(jnp.int32, (c2, c2), 1)).astype(F32)
    nchunk = blk // chunk

    pairs = []
    for pi in range(npp):
        ls = slice(pi * LANES, (pi + 1) * LANES)
        r = shift_lerp(3 * pi, r_ref[:, ls], mur_ref[:, ls])
        k = shift_lerp(3 * pi + 1, k_ref[:, ls], muk_ref[:, ls])
        v = shift_lerp(3 * pi + 2, v_ref[:, ls], muv_ref[:, ls])
        pre_w = -(w0_ref[:, ls] + _dot(tanh_wa, w2_ref[:, ls]))
        w_log = -(jnp.maximum(pre_w, 0.0) + jnp.log1p(jnp.exp(-jnp.abs(pre_w)))) - 0.5
        lw = -jnp.exp(w_log)
        a = _sigmoid(a0_ref[:, ls] + _dot(wa, a2_ref[:, ls]))
        g = _dot(sig_xg, g2_ref[:, ls])
        kk = k * kk_ref[:, ls]
        kk = kk / jnp.maximum(jnp.sqrt(head_sum(kk * kk)), 1e-12)
        k = k * (1.0 + (a - 1.0) * ka_ref[:, ls])
        bonus = head_sum(r * k * rk_ref[:, ls]) * v
        av = -kk
        bv = kk * a
        lcum = _dot_exact_lhs(tri_incl, lw)
        lsum = _dot_exact_lhs(same, lw)
        a_t = av * jnp.exp(lcum - lw)
        r_t = r * jnp.exp(lcum)
        inv = jnp.exp(-lcum)
        b_t = bv * inv
        k_t = k * inv
        to_end = jnp.exp(lsum - lcum)
        ars, amats = [], []
        for c in range(nchunk):
            rows = slice(c * chunk, (c + 1) * chunk)
            ar = jnp.concatenate([stack(a_t[rows]), stack(r_t[rows])], 0)
            bk = jnp.concatenate([stack(b_t[rows]), stack(k_t[rows])], 0)
            ars.append(ar)
            amats.append(jnp.where(amask, _dot(ar, bk, NT), 0.0))
        pairs.append(dict(v=v, g=g, bonus=bonus, b_c=bv * to_end, k_c=k * to_end, gam=jnp.exp(lsum),
                          ars=ars, amats=amats))

    keys = [(pi, c) for pi in range(npp) for c in range(nchunk)]
    xs = {key: pairs[key[0]]["amats"][key[1]][:c2, :c2] for key in keys}
    t_inv = {key: eye + xs[key] for key in keys}
    for _ in range(chunk.bit_length() - 2):
        for key in keys:
            xs[key] = _dot(xs[key], xs[key])
        for key in keys:
            t_inv[key] = t_inv[key] + _dot(t_inv[key], xs[key])

    sts = [s_state[pi] for pi in range(npp)]
    ys = [[] for _ in range(npp)]
    for c in range(nchunk):
        rows = slice(c * chunk, (c + 1) * chunk)
        for pi in range(npp):
            pr = pairs[pi]
            ar, amat = pr["ars"][c], pr["amats"][c]
            v_c = pr["v"][rows]
            v2 = jnp.concatenate([v_c, v_c], 0)
            g1 = _dot(ar, sts[pi], NT)
            u2 = _dot(t_inv[(pi, c)], g1[:c2] + _dot(amat[:c2, c2:], v2))
            y2 = g1[c2:] + _dot(amat[c2:], jnp.concatenate([u2, v2], 0))
            ys[pi].append(jnp.where(h1, y2[chunk:], y2[:chunk]))
            u = jnp.where(h1, u2[chunk:], u2[:chunk])
            upd = _dot(jnp.concatenate([u, v_c], 0), jnp.concatenate([pr["b_c"][rows], pr["k_c"][rows]], 0), TN)
            sts[pi] = jnp.where(head_bd, sts[pi] * pr["gam"][c * chunk:c * chunk + 1] + upd, 0.0)

    for pi in range(npp):
        ls = slice(pi * LANES, (pi + 1) * LANES)
        s_state[pi] = sts[pi]
        y = jnp.concatenate(ys[pi], 0)
        mu = head_sum(y) * (1.0 / HEAD_DIM)
        yc = y - mu
        var = head_sum(yc * yc) * (1.0 / HEAD_DIM)
        y = yc * lax.rsqrt(var + A_GN_EPS) * lnw_ref[:, ls] + lnb_ref[:, ls] + pairs[pi]["bonus"]
        o_ref[:, ls] = y * pairs[pi]["g"]


def _rwkv7(p, b, s, mu, w0, w2, a0, a2, g2, k_k, k_a, r_k, lnx_w, lnx_b, blk=SEQ_BLOCK, chunk=RW_CHUNK):
    blk = min(blk, s)
    nblk = s // blk
    npp = RW_PAIRS
    wide = npp * LANES
    ngrp = A_WIDTH // wide
    lora = 3 * A_WIDTH // LANES
    act = lambda o: pl.BlockSpec((blk, wide), lambda bi, hp, n: (bi * nblk + n, o * ngrp + hp))
    act_fix = lambda o: pl.BlockSpec((blk, LANES), lambda bi, hp, n: (bi * nblk + n, o))
    vec = lambda o: pl.BlockSpec((1, wide), lambda bi, hp, n: (0, o * ngrp + hp))
    vec_fix = lambda o: pl.BlockSpec((1, LANES), lambda bi, hp, n: (0, o))
    mat = pl.BlockSpec((LANES, wide), lambda bi, hp, n: (0, hp))
    row = lambda t: t.reshape(1, -1)
    w2p = jnp.concatenate([w2, jnp.zeros_like(a2)], 0)
    a2p = jnp.concatenate([jnp.zeros_like(w2), a2], 0)
    mu2 = row(mu)
    return pl.pallas_call(
        functools.partial(_rwkv7_kernel, blk=blk, chunk=chunk, npp=npp),
        name="rwkv7",
        grid=(b, ngrp, nblk),
        in_specs=[act(0), act(1), act(2), act_fix(lora), act_fix(lora + 1),
                  vec(0), vec(1), vec(2), vec_fix(lora), vec_fix(lora + 1),
                  vec(0), vec(0), vec(0), vec(0), vec(0), vec(0), vec(0), mat, mat, mat],
        out_specs=pl.BlockSpec((blk, wide), lambda bi, hp, n: (bi * nblk + n, hp)),
        out_shape=jax.ShapeDtypeStruct((b * s, A_WIDTH), F32),
        scratch_shapes=[pltpu.VMEM((8, (3 * npp + 2) * LANES), F32), pltpu.VMEM((npp, LANES, LANES), F32)],
        compiler_params=_params("parallel", "parallel", "arbitrary"),
    )(p, p, p, p, p, mu2, mu2, mu2, mu2, mu2,
      row(w0), row(a0), row(k_k), row(k_a), row(r_k), row(lnx_w), row(lnx_b), w2p, a2p, g2)


def kernel(x, ev_w_in, ev_w_out, rwkv_mu, rwkv_w0, rwkv_w2, rwkv_a0, rwkv_a2, rwkv_g2, rwkv_kk, rwkv_ka, rwkv_rk,
           rwkv_lnx_w, rwkv_lnx_b, hgrn_lb, hgrn_norm_w, od_w_in, od_w_out, mlstm_conv_w, mlstm_conv_b,
           mlstm_b_i, mlstm_b_f, ln_w, ln_b, moe_wg, moe_bg, moe_we, moe_be, moe_w1, moe_w3, moe_w2):
    b, s, d = x.shape
    depth = ln_w.shape[0]
    alpha = (2 * depth) ** 0.25
    lb_all = jnp.cumsum(jax.nn.softmax(hgrn_lb.astype(F32), axis=0), axis=0)
    lb_all = lb_all - lb_all[:1]
    tn = 2 * LANES
    xf = x.reshape(b * s, d)
    for layer in range(depth):
        j = layer // 2
        if layer % 2 == 0:
            p = _matmul(xf, ev_w_in[j].astype(BF16), tn=tn)
            ya = _rwkv7(p, b, s, rwkv_mu[j], rwkv_w0[j], rwkv_w2[j], rwkv_a0[j], rwkv_a2[j], rwkv_g2[j],
                        rwkv_kk[j], rwkv_ka[j], rwkv_rk[j], rwkv_lnx_w[j], rwkv_lnx_b[j])
            yb = _hgrn2(p, b, s, A_COLS // LANES, lb_all[j], hgrn_norm_w[j])
            w_out = ev_w_out[j]
        else:
            w_in = od_w_in[j]
            w_in = jnp.pad(w_in, ((0, 0), (0, -w_in.shape[1] % tn))).astype(BF16)
            p = _matmul(xf, w_in, tn=tn)
            nq = C_WIDTH // LANES
            ya = _dilated_attention(p, b, s, 0, nq, 2 * nq)
            yb = _mlstm(p, b, s, 3 * nq, mlstm_conv_w[j], mlstm_conv_b[j], mlstm_b_i[j], mlstm_b_f[j])
            w_out = od_w_out[j]
        xf = _proj_ln(ya, yb, w_out, xf, ln_w[layer, 0], ln_b[layer, 0], alpha)
        xf = _moe_sublayer(xf, moe_wg[layer], moe_bg[layer], moe_we[layer], moe_be[layer],
                           moe_w1, moe_w3, moe_w2, layer, ln_w[layer, 1], ln_b[layer, 1], alpha)
    return xf.reshape(b, s, d)
```

```python
import functools

import jax
import jax.numpy as jnp
from jax import lax
from jax.experimental import pallas as pl
from jax.experimental.pallas import tpu as pltpu

F32 = jnp.float32
BF16 = jnp.bfloat16

HEAD_DIM = 64
A_HEADS = 16
A_WIDTH = A_HEADS * HEAD_DIM
A_LORA_W = 64
A_LORA_A = 64
A_LORA_G = 128
A_COLS = 3 * A_WIDTH + A_LORA_W + A_LORA_A + A_LORA_G
A_GN_EPS = 64e-5
B_HEADS = 8
B_DK = 128
B_DV = 128
B_WIDTH = B_HEADS * B_DV
C_HEADS = 16
C_WIDTH = C_HEADS * HEAD_DIM
C_PAIRS = ((128, 1), (512, 4), (2048, 16))
D_HEADS = 4
D_DQK = 128
D_DV = 256
D_QK_WIDTH = D_HEADS * D_DQK
D_WIDTH = D_HEADS * D_DV
D_CONV = 4
ODD_COLS = 3 * C_WIDTH + 2 * D_QK_WIDTH + 2 * D_WIDTH + 2 * D_HEADS
N_GROUPS = 4
EXPERTS_PER_GROUP = 8
N_EXPERTS = N_GROUPS * EXPERTS_PER_GROUP
LN_EPS = 1e-5
RMS_EPS = 1e-6

LANES = 128
VMEM_LIMIT = 56 * 1024 * 1024
RW_CHUNK = 64
RW_PAIRS = 4
GLA_WIDE_CHUNK = 64
GLA_SAFE_LOG = 60.0
GLA_CHUNK = 16
ML_CHUNK = 64
SEQ_BLOCK = 256
ATT_BLOCK = 128
ATT_GROUP = 8
PROJ_SUB = 128
MOE_BM = 256
TOK_BITS = 15

NN = (((1,), (0,)), ((), ()))
NT = (((1,), (1,)), ((), ()))


def _dot(a, b, dims=NN):
    return lax.dot_general(a.astype(BF16), b.astype(BF16), dims, preferred_element_type=F32)


def _split3(x):
    hi = x.astype(BF16)
    r1 = x - hi.astype(F32)
    mid = r1.astype(BF16)
    lo = (r1 - mid.astype(F32)).astype(BF16)
    return hi, mid, lo


def _dot_exact_lhs(mask, x, dims=NN):
    hi, mid, lo = _split3(x)
    m = mask.astype(BF16)
    d = lambda p: lax.dot_general(m, p, dims, preferred_element_type=F32)
    return d(hi) + d(mid) + d(lo)


def _dot_exact_rhs(x, mask):
    hi, mid, lo = _split3(x)
    m = mask.astype(BF16)
    d = lambda p: lax.dot_general(p, m, NN, preferred_element_type=F32)
    return d(hi) + d(mid) + d(lo)


def _sigmoid(x):
    return 1.0 / (1.0 + jnp.exp(-x))


def _silu(x):
    return x * _sigmoid(x)


def _log_sigmoid(x):
    return jnp.minimum(x, 0.0) - jnp.log1p(jnp.exp(-jnp.abs(x)))


def _params(*sem):
    return pltpu.CompilerParams(dimension_semantics=sem, vmem_limit_bytes=VMEM_LIMIT)


def _mm_kernel(x_ref, w_ref, o_ref):
    o_ref[...] = _dot(x_ref[...], w_ref[...])


def _matmul(x, w, tm=2048, tn=256):
    t, k = x.shape
    n = w.shape[1]
    tm = min(tm, t)
    return pl.pallas_call(
        _mm_kernel,
        name="in_proj",
        grid=(t // tm, n // tn),
        in_specs=[pl.BlockSpec((tm, k), lambda i, j: (i, 0)),
                  pl.BlockSpec((k, tn), lambda i, j: (0, j))],
        out_specs=pl.BlockSpec((tm, tn), lambda i, j: (i, j)),
        out_shape=jax.ShapeDtypeStruct((t, n), F32),
        compiler_params=_params("parallel", "arbitrary"),
    )(x, w)


def _layer_norm(y, w, b):
    mu = jnp.mean(y, -1, keepdims=True)
    yc = y - mu
    var = jnp.mean(yc * yc, -1, keepdims=True)
    return yc * lax.rsqrt(var + LN_EPS) * w + b


def _proj_ln_kernel(ya_ref, yb_ref, wa_ref, wb_ref, x_ref, g_ref, b_ref, rh_ref, rm_ref, rb_ref,
                    o_ref, slab_ref, *, alpha):
    sub = min(PROJ_SUB, o_ref.shape[0])
    for i in range(o_ref.shape[0] // sub):
        rows = slice(i * sub, (i + 1) * sub)
        y = _dot(ya_ref[rows, :], wa_ref[...]) + _dot(yb_ref[rows, :], wb_ref[...]) + alpha * x_ref[rows, :]
        o_ref[rows, :] = _layer_norm(y, g_ref[...], b_ref[...])
    slab_ref[...] = _route(o_ref[...], rh_ref[...], rm_ref[...], rb_ref[...])


def _proj_ln(ya, yb, w_out, x, g, b, router, alpha, tm=512):
    t, d = x.shape
    ka = ya.shape[1]
    kb = yb.shape[1]
    tm = min(tm, t)
    wa = w_out[:ka].astype(BF16)
    wb = w_out[ka:].astype(BF16)
    row = lambda i: (i, 0)
    fix = lambda i: (0, 0)
    return pl.pallas_call(
        functools.partial(_proj_ln_kernel, alpha=alpha),
        name="proj_ln",
        grid=(t // tm,),
        in_specs=[pl.BlockSpec((tm, ka), row), pl.BlockSpec((tm, kb), row),
                  pl.BlockSpec((ka, d), fix), pl.BlockSpec((kb, d), fix),
                  pl.BlockSpec((tm, d), row), pl.BlockSpec((1, d), fix), pl.BlockSpec((1, d), fix),
                  pl.BlockSpec((d, LANES), fix), pl.BlockSpec((d, LANES), fix), pl.BlockSpec((1, LANES), fix)],
        out_specs=[pl.BlockSpec((tm, d), row), pl.BlockSpec((tm, LANES), row)],
        out_shape=[jax.ShapeDtypeStruct((t, d), F32), jax.ShapeDtypeStruct((t, LANES), F32)],
        compiler_params=_params("parallel"),
    )(ya, yb, wa, wb, x, g.reshape(1, d), b.reshape(1, d), *router)


def _route(x, wh, wm, bias):
    xh, xm, _ = _split3(x)
    d = lambda a, w: lax.dot_general(a, w, NN, preferred_element_type=F32)
    logits = d(xh, wh) + d(xh, wm) + d(xm, wh) + bias
    tm = logits.shape[0]
    lane = lax.broadcasted_iota(jnp.int32, (tm, LANES), 1)
    neg = -jnp.inf
    gl = jnp.where(lane < N_GROUPS, logits, neg)
    gmax = jnp.max(gl, -1, keepdims=True)
    g_top = jnp.min(jnp.where(gl == gmax, lane, LANES), -1, keepdims=True)
    p_group = 1.0 / jnp.sum(jnp.exp(gl - gmax), -1, keepdims=True)
    e_lane = lane - N_GROUPS
    in_group = (e_lane >= 0) & (e_lane < N_EXPERTS) & ((e_lane // EXPERTS_PER_GROUP) == g_top)
    el = jnp.where(in_group, logits, neg)
    v1 = jnp.max(el, -1, keepdims=True)
    i1 = jnp.min(jnp.where(el == v1, lane, LANES), -1, keepdims=True)
    el2 = jnp.where(lane == i1, neg, el)
    v2 = jnp.max(el2, -1, keepdims=True)
    i2 = jnp.min(jnp.where(el2 == v2, lane, LANES), -1, keepdims=True)
    e2 = jnp.exp(v2 - v1)
    w1 = 1.0 / (1.0 + e2)
    w2 = e2 / (1.0 + e2)
    return jnp.where(lane == 0, (i1 - N_GROUPS).astype(F32),
                     jnp.where(lane == 1, (i2 - N_GROUPS).astype(F32),
                               jnp.where(lane == 2, p_group * w1,
                                         jnp.where(lane == 3, p_group * w2, 0.0))))


def _router_weights(wg, bg, we, be):
    d = wg.shape[0]
    w = jnp.concatenate([wg, jnp.transpose(we, (1, 0, 2)).reshape(d, N_EXPERTS)], 1)
    w = jnp.pad(w, ((0, 0), (0, LANES - w.shape[1])))
    bias = jnp.pad(jnp.concatenate([bg, be.reshape(-1)]), (0, LANES - N_GROUPS - N_EXPERTS)).reshape(1, LANES)
    wh = w.astype(BF16)
    wm = (w - wh.astype(F32)).astype(BF16)
    return wh, wm, bias


def _ffn_kernel(bexp_ref, nvalid_ref, pk_ref, x_hbm, w1_ref, w3_ref, w2_ref, y_hbm,
                xbuf, ybuf, w1b, w3b, w2b, gsem, ssem, *, bm):
    j = pl.program_id(0)
    nb = pl.num_programs(0)
    nvalid = nvalid_ref[0]
    slot = j % 2

    @pl.when((j < nvalid) & ((j == 0) | (bexp_ref[j] != bexp_ref[jnp.maximum(j - 1, 0)])))
    def _():
        w1b[...] = w1_ref[...].astype(BF16)
        w3b[...] = w3_ref[...].astype(BF16)
        w2b[...] = w2_ref[...].astype(BF16)

    def row_copy_in(blk, s, r):
        src = pk_ref[blk * bm + r] & ((1 << TOK_BITS) - 1)
        return pltpu.make_async_copy(x_hbm.at[pl.ds(src, 1)], xbuf.at[s, pl.ds(r, 1)], gsem.at[s])

    def row_copy_out(blk, s, r):
        dst = lax.shift_right_logical(pk_ref[blk * bm + r], TOK_BITS)
        return pltpu.make_async_copy(ybuf.at[s, pl.ds(r, 1)], y_hbm.at[pl.ds(dst, 1)], ssem.at[s])

    def for_rows(fn):
        def body(r, c):
            fn(r)
            return c
        lax.fori_loop(0, bm, body, 0, unroll=8)

    @pl.when((j == 0) & (nvalid > 0))
    def _():
        for_rows(lambda r: row_copy_in(0, 0, r).start())

    @pl.when(j + 1 < nvalid)
    def _():
        for_rows(lambda r: row_copy_in(j + 1, 1 - slot, r).start())

    def wait_gather(s):
        pltpu.make_async_copy(xbuf.at[s], xbuf.at[s], gsem.at[s]).wait()

    def wait_scatter(s):
        pltpu.make_async_copy(ybuf.at[s], ybuf.at[s], ssem.at[s]).wait()

    @pl.when((j >= 2) & (j - 2 < nvalid))
    def _():
        wait_scatter(slot)

    @pl.when(j < nvalid)
    def _():
        wait_gather(slot)
        xb = xbuf[slot]
        h = _silu(_dot(xb, w1b[...])) * _dot(xb, w3b[...])
        ybuf[slot] = _dot(h, w2b[...])
        for_rows(lambda r: row_copy_out(j, slot, r).start())

    @pl.when(j >= nvalid)
    def _():
        ybuf[slot] = jnp.zeros((bm, ybuf.shape[2]), F32)
        fill = pltpu.make_async_copy(ybuf.at[slot], y_hbm.at[pl.ds(j * bm, bm)], ssem.at[slot])
        fill.start()
        fill.wait()

    @pl.when(j == nb - 1)
    def _():
        @pl.when((j >= 1) & (j - 1 < nvalid))
        def _():
            wait_scatter(1 - slot)

        @pl.when(j < nvalid)
        def _():
            wait_scatter(slot)


def _moe_dispatch(slab, t, bm):
    n = 2 * t
    e = N_EXPERTS
    rows = -(-(n + e * (bm - 1)) // bm) * bm
    ids = slab[:, :2].astype(jnp.int32).reshape(n)
    order = jnp.argsort(ids, stable=True).astype(jnp.int32)
    counts = jnp.sum((ids[:, None] == jnp.arange(e, dtype=jnp.int32)[None, :]).astype(jnp.int32), 0)
    padded = -(-counts // bm) * bm
    pend = jnp.cumsum(padded)
    pstart = pend - padded
    start = jnp.cumsum(counts) - counts
    n_blocks = rows // bm
    bstart = jnp.arange(n_blocks, dtype=jnp.int32) * bm
    block_exp = jnp.minimum(jnp.sum((bstart[:, None] >= pend[None, :]).astype(jnp.int32), 1), e - 1)
    nvalid = (pend[-1] // bm).astype(jnp.int32).reshape(1)
    per_row = lambda tbl: jnp.repeat(tbl[block_exp], bm)
    local = jnp.arange(rows, dtype=jnp.int32) - per_row(pstart)
    valid = local < per_row(counts)
    a = order[jnp.clip(per_row(start) + local, 0, n - 1)]
    pad_rank = jnp.cumsum(1 - valid.astype(jnp.int32)) - 1
    row_src = jnp.where(valid, a // 2, 0)
    row_dst = jnp.where(valid, (a % 2) * t + a // 2, n + pad_rank)
    packed = row_src | (row_dst << TOK_BITS)
    return packed.astype(jnp.int32), block_exp, nvalid, rows


def _moe_ffn(x, slab, w1, w3, w2, layer, bm=MOE_BM):
    t, d = x.shape
    de = w1.shape[3]
    assert t <= (1 << TOK_BITS)
    packed, block_exp, nvalid, rows = _moe_dispatch(slab, t, bm)
    assert rows < (1 << (32 - TOK_BITS))
    n_blocks = rows // bm
    wmap = lambda j, bexp, nv, pk: (layer, bexp[j], 0, 0)
    grid_spec = pltpu.PrefetchScalarGridSpec(
        num_scalar_prefetch=3,
        grid=(n_blocks,),
        in_specs=[pl.BlockSpec(memory_space=pl.ANY),
                  pl.BlockSpec((None, None, d, de), wmap), pl.BlockSpec((None, None, d, de), wmap),
                  pl.BlockSpec((None, None, de, d), wmap)],
        out_specs=pl.BlockSpec(memory_space=pl.ANY),
        scratch_shapes=[pltpu.VMEM((2, bm, d), F32), pltpu.VMEM((2, bm, d), F32),
                        pltpu.VMEM((d, de), BF16), pltpu.VMEM((d, de), BF16), pltpu.VMEM((de, d), BF16),
                        pltpu.SemaphoreType.DMA((2,)), pltpu.SemaphoreType.DMA((2,))],
    )
    return pl.pallas_call(
        functools.partial(_ffn_kernel, bm=bm),
        name="moe_ffn",
        grid_spec=grid_spec,
        out_shape=jax.ShapeDtypeStruct((rows, d), F32),
        compiler_params=_params("arbitrary"),
    )(block_exp, nvalid, packed, x, w1, w3, w2)


def _moe_ln_kernel(x_ref, y0_ref, y1_ref, s_ref, g_ref, b_ref, o_ref, oh_ref, *, alpha):
    s = s_ref[...]
    y = alpha * x_ref[...] + s[:, 2:3] * y0_ref[...] + s[:, 3:4] * y1_ref[...]
    o = _layer_norm(y, g_ref[...], b_ref[...])
    o_ref[...] = o
    oh_ref[...] = o.astype(BF16)


def _moe_ln(x, y, slab, g, b, alpha, tm=512):
    t, d = x.shape
    tm = min(tm, t)
    nt = t // tm
    row = lambda i: (i, 0)
    fix = lambda i: (0, 0)
    return pl.pallas_call(
        functools.partial(_moe_ln_kernel, alpha=alpha),
        name="moe_ln",
        grid=(nt,),
        in_specs=[pl.BlockSpec((tm, d), row), pl.BlockSpec((tm, d), row),
                  pl.BlockSpec((tm, d), lambda i: (i + nt, 0)), pl.BlockSpec((tm, LANES), row),
                  pl.BlockSpec((1, d), fix), pl.BlockSpec((1, d), fix)],
        out_specs=[pl.BlockSpec((tm, d), row), pl.BlockSpec((tm, d), row)],
        out_shape=[jax.ShapeDtypeStruct((t, d), F32), jax.ShapeDtypeStruct((t, d), BF16)],
        compiler_params=_params("parallel"),
    )(x, y, y, slab, g.reshape(1, d), b.reshape(1, d))


def _moe_sublayer(x, slab, w1, w3, w2, layer, g, b, alpha):
    y = _moe_ffn(x, slab, w1, w3, w2, layer)
    return _moe_ln(x, y, slab, g, b, alpha)


def _attn_kernel(q_ref, kp_ref, kc_ref, vp_ref, vc_ref, o_ref, k_all, v_all, q_all, acc_o, acc_m, acc_l, *, rows):
    n = pl.program_id(2)
    k_all[0:rows, :] = kp_ref[...]
    k_all[rows:2 * rows, :] = kc_ref[...]
    v_all[0:rows, :] = vp_ref[...]
    v_all[rows:2 * rows, :] = vc_ref[...]
    q_all[...] = q_ref[...] * (HEAD_DIM ** -0.5)
    nk = ATT_BLOCK
    qi = lax.broadcasted_iota(jnp.int32, (nk, 2 * nk), 0)
    kj = lax.broadcasted_iota(jnp.int32, (nk, 2 * nk), 1)
    band = (kj >= qi) & (kj <= qi + nk)
    lane = lax.broadcasted_iota(jnp.int32, (nk, LANES), 1)
    head1 = lane >= HEAD_DIM

    for p_idx, (window, dil) in enumerate(C_PAIRS):
        assert window // dil == nk
        span = nk * dil
        n_sub = rows // span

        def group(gi, carry, dil=dil, span=span, n_sub=n_sub, first=(p_idx == 0)):
            sel = lambda ref, start: ref[pl.ds(start, nk, stride=dil), :]
            loaded = []
            for i in range(ATT_GROUP):
                u = gi * ATT_GROUP + i
                res = u // n_sub
                sub = u - res * n_sub
                qstart = sub * span + res
                q = sel(q_all, qstart)
                k2 = jnp.concatenate([sel(k_all, rows + qstart - span), sel(k_all, rows + qstart)], 0).astype(BF16)
                v2 = jnp.concatenate([sel(v_all, rows + qstart - span), sel(v_all, rows + qstart)], 0).astype(BF16)
                dst = pl.ds(qstart, nk, stride=dil)
                old = None if first else (acc_o[dst, :], acc_m[dst, :], acc_l[dst, :])
                loaded.append((q, k2, v2, dst, old, sub))
            results = []
            for q, k2, v2, dst, old, sub in loaded:
                ok = band & ((kj >= nk) | (n > 0) | (sub > 0))
                outs = []
                for h1 in (False, True):
                    hm = head1 if h1 else jnp.logical_not(head1)
                    sc = _dot(jnp.where(hm, q, 0.0), k2, NT)
                    sc = jnp.where(ok, sc, -jnp.inf)
                    m = jnp.max(sc, -1, keepdims=True)
                    p = jnp.exp(sc - m)
                    outs.append((_dot(p, v2), m, jnp.sum(p, -1, keepdims=True)))
                o = jnp.where(head1, outs[1][0], outs[0][0])
                m = jnp.where(head1, outs[1][1], outs[0][1])
                l = jnp.where(head1, outs[1][2], outs[0][2])
                if old is not None:
                    o_old, m_old, l_old = old
                    m_new = jnp.maximum(m_old, m)
                    a_old = jnp.exp(m_old - m_new)
                    a_cur = jnp.exp(m - m_new)
                    o = o_old * a_old + o * a_cur
                    l = l_old * a_old + l * a_cur
                    m = m_new
                results.append((dst, o, m, l))
            for dst, o, m, l in results:
                acc_o[dst, :] = o
                acc_m[dst, :] = m
                acc_l[dst, :] = l
            return carry

        assert (dil * n_sub) % ATT_GROUP == 0
        lax.fori_loop(0, dil * n_sub // ATT_GROUP, group, 0)

    o_ref[...] = acc_o[...] / acc_l[...]


def _dilated_attention(p, b, s, qoff, koff, voff):
    rows = max(w for w, _ in C_PAIRS)
    assert s % rows == 0
    nblk = s // rows
    cur = lambda off: pl.BlockSpec((rows, LANES), lambda bi, hp, n: (bi * nblk + n, off + hp))
    prev = lambda off: pl.BlockSpec((rows, LANES), lambda bi, hp, n: (bi * nblk + jnp.maximum(n - 1, 0), off + hp))
    scratch = [pltpu.VMEM((2 * rows, LANES), F32), pltpu.VMEM((2 * rows, LANES), F32)] + \
              [pltpu.VMEM((rows, LANES), F32)] * 4
    return pl.pallas_call(
        functools.partial(_attn_kernel, rows=rows),
        name="dil_attn",
        grid=(b, C_WIDTH // LANES, nblk),
        in_specs=[cur(qoff), prev(koff), cur(koff), prev(voff), cur(voff)],
        out_specs=pl.BlockSpec((rows, LANES), lambda bi, hp, n: (bi * nblk + n, hp)),
        out_shape=jax.ShapeDtypeStruct((b * s, C_WIDTH), F32),
        scratch_shapes=scratch,
        compiler_params=_params("parallel", "parallel", "arbitrary"),
    )(p, p, p, p, p)


def _causal_conv(x_ref, pad_ref, w_ref, b_ref, first, blk):
    @pl.when(first)
    def _():
        pad_ref[0:8, :] = jnp.zeros((8, pad_ref.shape[1]), F32)
    pad_ref[8:8 + blk, :] = x_ref[...]
    w = w_ref[...]
    acc = b_ref[...]
    for j in range(D_CONV):
        acc = acc + w[j:j + 1, :] * pad_ref[pl.ds(8 - (D_CONV - 1) + j, blk), :]
    pad_ref[0:8, :] = pad_ref[blk:blk + 8, :]
    return acc


def _mlstm_kernel(q_ref, k_ref, v_ref, og_ref, gate_ref, cwq_ref, cwk_ref, cbq_ref, cbk_ref, gb_ref,
                  o_ref, qpad, kpad, c_state, n_state, m_state, *, blk, chunk):
    first = pl.program_id(1) == 0

    @pl.when(first)
    def _():
        c_state[...] = jnp.zeros_like(c_state)
        n_state[...] = jnp.zeros_like(n_state)
        m_state[...] = jnp.zeros_like(m_state)

    qc = _silu(_causal_conv(q_ref, qpad, cwq_ref, cbq_ref, first, blk))
    kc = _silu(_causal_conv(k_ref, kpad, cwk_ref, cbk_ref, first, blk)) * (D_DQK ** -0.5)
    gates = gate_ref[...] + gb_ref[...]
    logf = _log_sigmoid(gates)
    lane = lax.broadcasted_iota(jnp.int32, (chunk, LANES), 1)
    ti = lax.broadcasted_iota(jnp.int32, (chunk, chunk), 0)
    si = lax.broadcasted_iota(jnp.int32, (chunk, chunk), 1)
    causal = si <= ti
    tri = causal.astype(F32)

    c_mat = [c_state[h] for h in range(D_HEADS)]
    n_vec = [n_state[h:h + 1, :] for h in range(D_HEADS)]
    m_prev = [m_state[h:h + 1, 0:1] for h in range(D_HEADS)]
    for c in range(blk // chunk):
        rows = slice(c * chunk, (c + 1) * chunk)
        gi = gates[rows]
        fcum_all = _dot_exact_lhs(tri, logf[rows])
        for h in range(D_HEADS):
            sel_i = lane == h
            sel_f = lane == D_HEADS + h
            qs = slice(h * D_DQK, (h + 1) * D_DQK)
            vs = slice(h * D_DV, (h + 1) * D_DV)
            q_c, k_c, v_c = qc[rows, qs], kc[rows, qs], v_ref[rows, vs]
            f_col = jnp.sum(jnp.where(sel_f, fcum_all, 0.0), -1, keepdims=True)
            i_col = jnp.sum(jnp.where(sel_i, gi, 0.0), -1, keepdims=True)
            f_row = _dot_exact_lhs(sel_f.astype(F32), fcum_all, NT)
            i_row = _dot_exact_lhs(sel_i.astype(F32), gi, NT)
            log_d = jnp.where(causal, f_col - f_row + i_row, -jnp.inf)
            log_inter = f_col + m_prev[h]
            m_t = jnp.maximum(jnp.max(log_d, -1, keepdims=True), log_inter)
            w_inter = jnp.exp(log_inter - m_t)
            scores = _dot(q_c, k_c, NT) * jnp.exp(log_d - m_t)
            num = w_inter * _dot(q_c, c_mat[h]) + _dot(scores, v_c)
            den = w_inter * jnp.sum(q_c * n_vec[h], -1, keepdims=True) + jnp.sum(scores, -1, keepdims=True)
            out = num / jnp.maximum(jnp.abs(den), jnp.exp(-m_t))
            o_ref[rows, vs] = _sigmoid(og_ref[rows, vs]) * out
            f_tot = f_col[chunk - 1:chunk]
            log_w = f_tot - f_col + i_col
            m_new = jnp.maximum(f_tot + m_prev[h], jnp.max(log_w, 0, keepdims=True))
            kw = k_c * jnp.exp(log_w - m_new)
            w_old = jnp.exp(f_tot + m_prev[h] - m_new)
            c_mat[h] = w_old * c_mat[h] + _dot(kw.T, v_c)
            n_vec[h] = w_old * n_vec[h] + jnp.sum(kw, 0, keepdims=True)
            m_prev[h] = m_new
    for h in range(D_HEADS):
        c_state[h] = c_mat[h]
        n_state[h:h + 1, :] = n_vec[h]
        m_state[h:h + 1, :] = jnp.broadcast_to(m_prev[h], (1, LANES))


def _mlstm(p, b, s, off, conv_w, conv_b, b_i, b_f, blk=SEQ_BLOCK, chunk=ML_CHUNK):
    blk = min(blk, s)
    nblk = s // blk
    hq = D_QK_WIDTH // LANES
    hv = D_WIDTH // LANES
    assert off % hq == 0 and (off + 2 * hq) % hv == 0
    rowblk = lambda bi, n: bi * nblk + n
    qk_spec = lambda o: pl.BlockSpec((blk, D_QK_WIDTH), lambda bi, n: (rowblk(bi, n), off // hq + o))
    v_spec = lambda o: pl.BlockSpec((blk, D_WIDTH), lambda bi, n: (rowblk(bi, n), (off + 2 * hq) // hv + o))
    par_spec = lambda rows, o: pl.BlockSpec((rows, D_QK_WIDTH), lambda bi, n: (0, o))
    gate_off = off + 2 * hq + 2 * hv
    gbias = jnp.pad(jnp.concatenate([b_i, b_f]), (0, LANES - 2 * D_HEADS)).reshape(1, LANES)
    cb = conv_b.reshape(1, -1)
    return pl.pallas_call(
        functools.partial(_mlstm_kernel, blk=blk, chunk=chunk),
        name="mlstm",
        grid=(b, nblk),
        in_specs=[qk_spec(0), qk_spec(1), v_spec(0), v_spec(1),
                  pl.BlockSpec((blk, LANES), lambda bi, n: (rowblk(bi, n), gate_off)),
                  par_spec(D_CONV, 0), par_spec(D_CONV, 1), par_spec(1, 0), par_spec(1, 1),
                  pl.BlockSpec((1, LANES), lambda bi, n: (0, 0))],
        out_specs=pl.BlockSpec((blk, D_WIDTH), lambda bi, n: (rowblk(bi, n), 0)),
        out_shape=jax.ShapeDtypeStruct((b * s, D_WIDTH), F32),
        scratch_shapes=[pltpu.VMEM((blk + 8, D_QK_WIDTH), F32), pltpu.VMEM((blk + 8, D_QK_WIDTH), F32),
                        pltpu.VMEM((D_HEADS, D_DQK, D_DV), F32), pltpu.VMEM((8, LANES), F32),
                        pltpu.VMEM((8, LANES), F32)],
        compiler_params=_params("parallel", "arbitrary"),
    )(p, p, p, p, p, conv_w, conv_w, cb, cb, gbias)


TN = (((0,), (0,)), ((), ()))


def _chunk_masks(blk, chunk):
    t = lax.broadcasted_iota(jnp.int32, (blk, blk), 0)
    s = lax.broadcasted_iota(jnp.int32, (blk, blk), 1)
    same = (t // chunk) == (s // chunk)
    return same, same & (s <= t), same & (s < t)


def _hgrn2_kernel(q_ref, f_ref, i_ref, g_ref, lb_ref, nw_ref, o_ref, s_state, o_scr, *, blk, chunk, wide):
    @pl.when(pl.program_id(2) == 0)
    def _():
        s_state[...] = jnp.zeros_like(s_state)

    lb = lb_ref[...]
    f = f_ref[...]
    q = _silu(q_ref[...])
    v = i_ref[...]
    a = jnp.log(lb)
    bb = jnp.log1p(-lb) + _log_sigmoid(f)
    log_f = jnp.maximum(a, bb) + jnp.log1p(jnp.exp(-jnp.abs(a - bb)))
    k = (1.0 - lb) * _sigmoid(-f)

    _, tri_w, _ = _chunk_masks(blk, wide)
    bw = _dot_exact_lhs(tri_w.astype(F32), log_f)
    wchunks = [bw[c * wide:(c + 1) * wide] for c in range(blk // wide)]
    mids = [bc[wide // 2 - 1:wide // 2] for bc in wchunks]
    spread = [jnp.max(jnp.abs(bc - m)) for bc, m in zip(wchunks, mids)]
    safe = functools.reduce(jnp.maximum, spread) < GLA_SAFE_LOG

    @pl.when(safe)
    def _():
        ti = lax.broadcasted_iota(jnp.int32, (wide, wide), 0)
        si = lax.broadcasted_iota(jnp.int32, (wide, wide), 1)
        st = s_state[...]
        for c, (bc, m) in enumerate(zip(wchunks, mids)):
            rows = slice(c * wide, (c + 1) * wide)
            q_c, k_c, v_c = q[rows], k[rows], v[rows]
            total = bc[wide - 1:wide]
            att = jnp.where(si <= ti, _dot(q_c * jnp.exp(bc - m), k_c * jnp.exp(m - bc), NT), 0.0)
            o_scr[rows, :] = _dot(att, v_c) + _dot(q_c * jnp.exp(bc), st, NT)
            st = st * jnp.exp(total) + _dot(v_c, k_c * jnp.exp(total - bc), TN)
        s_state[...] = st

    @pl.when(jnp.logical_not(safe))
    def _():
        same, tri_incl, _ = _chunk_masks(blk, chunk)
        bcum = _dot_exact_lhs(tri_incl.astype(F32), log_f)
        blast = _dot_exact_lhs(same.astype(F32), log_f)
        qe = q * jnp.exp(bcum)
        kd = k * jnp.exp(blast - bcum)
        decay = jnp.exp(blast)
        pos = lax.broadcasted_iota(jnp.int32, (chunk, LANES), 0)
        st = s_state[...]
        for c in range(blk // chunk):
            rows = slice(c * chunk, (c + 1) * chunk)
            q_c, k_c, v_c, b_c = q[rows], k[rows], v[rows], bcum[rows]
            o_c = _dot(qe[rows], st, NT) + jnp.sum(q_c * k_c, -1, keepdims=True) * v_c
            for dist in range(1, chunk):
                e = jnp.exp(jnp.where(pos >= dist, b_c - pltpu.roll(b_c, dist, 0), -jnp.inf))
                w = jnp.sum(q_c * pltpu.roll(k_c, dist, 0) * e, -1, keepdims=True)
                o_c = o_c + w * pltpu.roll(v_c, dist, 0)
            o_scr[rows, :] = o_c
            st = st * decay[c * chunk:c * chunk + 1] + _dot(v_c, kd[rows], TN)
        s_state[...] = st

    o = o_scr[...]
    o = o * lax.rsqrt(jnp.mean(o * o, -1, keepdims=True) + RMS_EPS) * nw_ref[...]
    o_ref[...] = o * _silu(g_ref[...])


def _hgrn2(p, b, s, off, lb, norm_w, blk=SEQ_BLOCK, chunk=GLA_CHUNK):
    blk = min(blk, s)
    nblk = s // blk
    spec = lambda o: pl.BlockSpec((blk, LANES), lambda bi, h, n: (bi * nblk + n, off + o * B_HEADS + h))
    return pl.pallas_call(
        functools.partial(_hgrn2_kernel, blk=blk, chunk=chunk, wide=min(GLA_WIDE_CHUNK, blk)),
        name="hgrn2",
        grid=(b, B_HEADS, nblk),
        in_specs=[spec(0), spec(1), spec(2), spec(3),
                  pl.BlockSpec((1, LANES), lambda bi, h, n: (0, h)),
                  pl.BlockSpec((1, LANES), lambda bi, h, n: (0, 0))],
        out_specs=pl.BlockSpec((blk, LANES), lambda bi, h, n: (bi * nblk + n, h)),
        out_shape=jax.ShapeDtypeStruct((b * s, B_WIDTH), F32),
        scratch_shapes=[pltpu.VMEM((B_DV, B_DK), F32), pltpu.VMEM((blk, LANES), F32)],
        compiler_params=_params("parallel", "parallel", "arbitrary"),
    )(p, p, p, p, lb.reshape(1, -1), norm_w.reshape(1, -1))


def _rwkv7_kernel(r_ref, k_ref, v_ref, wa_ref, xg_ref, mur_ref, muk_ref, muv_ref, muwa_ref, mug_ref,
                  w0_ref, a0_ref, kk_ref, ka_ref, rk_ref, lnw_ref, lnb_ref, w2_ref, a2_ref, g2_ref,
                  o_ref, tail, s_state, *, blk, chunk, npp):
    @pl.when(pl.program_id(2) == 0)
    def _():
        tail[...] = jnp.zeros_like(tail)
        s_state[...] = jnp.zeros_like(s_state)

    row0 = lax.broadcasted_iota(jnp.int32, (blk, LANES), 0) == 0

    def shift_lerp(idx, x, mu):
        prev = tail[0:1, idx * LANES:(idx + 1) * LANES]
        shifted = jnp.where(row0, prev, pltpu.roll(x, 1, 0))
        tail[0:1, idx * LANES:(idx + 1) * LANES] = x[blk - 1:blk]
        return x + (shifted - x) * mu

    wa = shift_lerp(3 * npp, wa_ref[...], muwa_ref[...])
    xg = shift_lerp(3 * npp + 1, xg_ref[...], mug_ref[...])
    tanh_wa = jnp.tanh(wa)
    sig_xg = _sigmoid(xg)

    lane = lax.broadcasted_iota(jnp.int32, (LANES, LANES), 1)
    sub = lax.broadcasted_iota(jnp.int32, (LANES, LANES), 0)
    head_bd = (lane // HEAD_DIM) == (sub // HEAD_DIM)
    head_sum = lambda x: _dot_exact_rhs(x, head_bd.astype(F32))
    same, tri_incl, _ = _chunk_masks(blk, chunk)
    same = same.astype(F32)
    tri_incl = tri_incl.astype(F32)
    c2 = 2 * chunk
    lane_c = lax.broadcasted_iota(jnp.int32, (chunk, LANES), 1)
    h1 = lane_c >= HEAD_DIM
    h0 = jnp.logical_not(h1)
    stack = lambda x: jnp.concatenate([jnp.where(h0, x, 0.0), jnp.where(h1, x, 0.0)], 0)
    ri = lax.broadcasted_iota(jnp.int32, (2 * c2, 2 * c2), 0)
    ci = lax.broadcasted_iota(jnp.int32, (2 * c2, 2 * c2), 1)
    amask = ((ci % chunk) < (ri % chunk)) | (((ci % chunk) == (ri % chunk)) & (ri >= c2))
    eye = (lax.broadcasted_iota(jnp.int32, (c2, c2), 0) == lax.broadcasted_iota(jnp.int32, (c2, c2), 1)).astype(F32)
    nchunk = blk // chunk

    pairs = []
    for pi in range(npp):
        ls = slice(pi * LANES, (pi + 1) * LANES)
        r = shift_lerp(3 * pi, r_ref[:, ls], mur_ref[:, ls])
        k = shift_lerp(3 * pi + 1, k_ref[:, ls], muk_ref[:, ls])
        v = shift_lerp(3 * pi + 2, v_ref[:, ls], muv_ref[:, ls])
        pre_w = -(w0_ref[:, ls] + _dot(tanh_wa, w2_ref[:, ls]))
        w_log = -(jnp.maximum(pre_w, 0.0) + jnp.log1p(jnp.exp(-jnp.abs(pre_w)))) - 0.5
        lw = -jnp.exp(w_log)
        a = _sigmoid(a0_ref[:, ls] + _dot(wa, a2_ref[:, ls]))
        g = _dot(sig_xg, g2_ref[:, ls])
        kk = k * kk_ref[:, ls]
        kk = kk / jnp.maximum(jnp.sqrt(head_sum(kk * kk)), 1e-12)
        k = k * (1.0 + (a - 1.0) * ka_ref[:, ls])
        bonus = head_sum(r * k * rk_ref[:, ls]) * v
        av = -kk
        bv = kk * a
        lcum = _dot_exact_lhs(tri_incl, lw)
        lsum = _dot_exact_lhs(same, lw)
        a_t = av * jnp.exp(lcum - lw)
        r_t = r * jnp.exp(lcum)
        inv = jnp.exp(-lcum)
        b_t = bv * inv
        k_t = k * inv
        to_end = jnp.exp(lsum - lcum)
        ars, amats = [], []
        for c in range(nchunk):
            rows = slice(c * chunk, (c + 1) * chunk)
            ar = jnp.concatenate([stack(a_t[rows]), stack(r_t[rows])], 0)
            bk = jnp.concatenate([stack(b_t[rows]), stack(k_t[rows])], 0)
            ars.append(ar)
            amats.append(jnp.where(amask, _dot(ar, bk, NT), 0.0))
        pairs.append(dict(v=v, g=g, bonus=bonus, b_c=bv * to_end, k_c=k * to_end, gam=jnp.exp(lsum),
                          ars=ars, amats=amats))

    keys = [(pi, c) for pi in range(npp) for c in range(nchunk)]
    xs = {key: pairs[key[0]]["amats"][key[1]][:c2, :c2] for key in keys}
    t_inv = {key: eye + xs[key] for key in keys}
    for _ in range(chunk.bit_length() - 2):
        for key in keys:
            xs[key] = _dot(xs[key], xs[key])
        for key in keys:
            t_inv[key] = t_inv[key] + _dot(t_inv[key], xs[key])

    sts = [s_state[pi] for pi in range(npp)]
    ys = [[] for _ in range(npp)]
    for c in range(nchunk):
        rows = slice(c * chunk, (c + 1) * chunk)
        for pi in range(npp):
            pr = pairs[pi]
            ar, amat = pr["ars"][c], pr["amats"][c]
            v_c = pr["v"][rows]
            v2 = jnp.concatenate([v_c, v_c], 0)
            g1 = _dot(ar, sts[pi], NT)
            u2 = _dot(t_inv[(pi, c)], g1[:c2] + _dot(amat[:c2, c2:], v2))
            y2 = g1[c2:] + _dot(amat[c2:], jnp.concatenate([u2, v2], 0))
            ys[pi].append(jnp.where(h1, y2[chunk:], y2[:chunk]))
            u = jnp.where(h1, u2[chunk:], u2[:chunk])
            upd = _dot(jnp.concatenate([u, v_c], 0), jnp.concatenate([pr["b_c"][rows], pr["k_c"][rows]], 0), TN)
            sts[pi] = jnp.where(head_bd, sts[pi] * pr["gam"][c * chunk:c * chunk + 1] + upd, 0.0)

    for pi in range(npp):
        ls = slice(pi * LANES, (pi + 1) * LANES)
        s_state[pi] = sts[pi]
        y = jnp.concatenate(ys[pi], 0)
        mu = head_sum(y) * (1.0 / HEAD_DIM)
        yc = y - mu
        var = head_sum(yc * yc) * (1.0 / HEAD_DIM)
        y = yc * lax.rsqrt(var + A_GN_EPS) * lnw_ref[:, ls] + lnb_ref[:, ls] + pairs[pi]["bonus"]
        o_ref[:, ls] = y * pairs[pi]["g"]


def _rwkv7(p, b, s, mu, w0, w2, a0, a2, g2, k_k, k_a, r_k, lnx_w, lnx_b, blk=SEQ_BLOCK, chunk=RW_CHUNK):
    blk = min(blk, s)
    nblk = s // blk
    npp = RW_PAIRS
    wide = npp * LANES
    ngrp = A_WIDTH // wide
    lora = 3 * A_WIDTH // LANES
    act = lambda o: pl.BlockSpec((blk, wide), lambda bi, hp, n: (bi * nblk + n, o * ngrp + hp))
    act_fix = lambda o: pl.BlockSpec((blk, LANES), lambda bi, hp, n: (bi * nblk + n, o))
    vec = lambda o: pl.BlockSpec((1, wide), lambda bi, hp, n: (0, o * ngrp + hp))
    vec_fix = lambda o: pl.BlockSpec((1, LANES), lambda bi, hp, n: (0, o))
    mat = pl.BlockSpec((LANES, wide), lambda bi, hp, n: (0, hp))
    row = lambda t: t.reshape(1, -1)
    w2p = jnp.concatenate([w2, jnp.zeros_like(a2)], 0)
    a2p = jnp.concatenate([jnp.zeros_like(w2), a2], 0)
    mu2 = row(mu)
    return pl.pallas_call(
        functools.partial(_rwkv7_kernel, blk=blk, chunk=chunk, npp=npp),
        name="rwkv7",
        grid=(b, ngrp, nblk),
        in_specs=[act(0), act(1), act(2), act_fix(lora), act_fix(lora + 1),
                  vec(0), vec(1), vec(2), vec_fix(lora), vec_fix(lora + 1),
                  vec(0), vec(0), vec(0), vec(0), vec(0), vec(0), vec(0), mat, mat, mat],
        out_specs=pl.BlockSpec((blk, wide), lambda bi, hp, n: (bi * nblk + n, hp)),
        out_shape=jax.ShapeDtypeStruct((b * s, A_WIDTH), F32),
        scratch_shapes=[pltpu.VMEM((8, (3 * npp + 2) * LANES), F32), pltpu.VMEM((npp, LANES, LANES), F32)],
        compiler_params=_params("parallel", "parallel", "arbitrary"),
    )(p, p, p, p, p, mu2, mu2, mu2, mu2, mu2,
      row(w0), row(a0), row(k_k), row(k_a), row(r_k), row(lnx_w), row(lnx_b), w2p, a2p, g2)


def kernel(x, ev_w_in, ev_w_out, rwkv_mu, rwkv_w0, rwkv_w2, rwkv_a0, rwkv_a2, rwkv_g2, rwkv_kk, rwkv_ka, rwkv_rk,
           rwkv_lnx_w, rwkv_lnx_b, hgrn_lb, hgrn_norm_w, od_w_in, od_w_out, mlstm_conv_w, mlstm_conv_b,
           mlstm_b_i, mlstm_b_f, ln_w, ln_b, moe_wg, moe_bg, moe_we, moe_be, moe_w1, moe_w3, moe_w2):
    b, s, d = x.shape
    depth = ln_w.shape[0]
    alpha = (2 * depth) ** 0.25
    lb_all = jnp.cumsum(jax.nn.softmax(hgrn_lb.astype(F32), axis=0), axis=0)
    lb_all = lb_all - lb_all[:1]
    tn = 2 * LANES
    xf = x.reshape(b * s, d)
    xh = xf
    for layer in range(depth):
        j = layer // 2
        if layer % 2 == 0:
            p = _matmul(xh, ev_w_in[j].astype(BF16), tn=tn)
            ya = _rwkv7(p, b, s, rwkv_mu[j], rwkv_w0[j], rwkv_w2[j], rwkv_a0[j], rwkv_a2[j], rwkv_g2[j],
                        rwkv_kk[j], rwkv_ka[j], rwkv_rk[j], rwkv_lnx_w[j], rwkv_lnx_b[j])
            yb = _hgrn2(p, b, s, A_COLS // LANES, lb_all[j], hgrn_norm_w[j])
            w_out = ev_w_out[j]
        else:
            w_in = od_w_in[j]
            w_in = jnp.pad(w_in, ((0, 0), (0, -w_in.shape[1] % tn))).astype(BF16)
            p = _matmul(xh, w_in, tn=tn)
            nq = C_WIDTH // LANES
            ya = _dilated_attention(p, b, s, 0, nq, 2 * nq)
            yb = _mlstm(p, b, s, 3 * nq, mlstm_conv_w[j], mlstm_conv_b[j], mlstm_b_i[j], mlstm_b_f[j])
            w_out = od_w_out[j]
        router = _router_weights(moe_wg[layer], moe_bg[layer], moe_we[layer], moe_be[layer])
        xf, slab = _proj_ln(ya, yb, w_out, xf, ln_w[layer, 0], ln_b[layer, 0], router, alpha)
        xf, xh = _moe_sublayer(xf, slab, moe_w1, moe_w3, moe_w2, layer, ln_w[layer, 1], ln_b[layer, 1], alpha)
    return xf.reshape(b, s, d)
```

```python
import functools

import jax
import jax.numpy as jnp
from jax import lax
from jax.experimental import pallas as pl
from jax.experimental.pallas import tpu as pltpu

F32 = jnp.float32
BF16 = jnp.bfloat16

HEAD_DIM = 64
A_HEADS = 16
A_WIDTH = A_HEADS * HEAD_DIM
A_LORA_W = 64
A_LORA_A = 64
A_LORA_G = 128
A_COLS = 3 * A_WIDTH + A_LORA_W + A_LORA_A + A_LORA_G
A_GN_EPS = 64e-5
B_HEADS = 8
B_DK = 128
B_DV = 128
B_WIDTH = B_HEADS * B_DV
C_HEADS = 16
C_WIDTH = C_HEADS * HEAD_DIM
C_PAIRS = ((128, 1), (512, 4), (2048, 16))
D_HEADS = 4
D_DQK = 128
D_DV = 256
D_QK_WIDTH = D_HEADS * D_DQK
D_WIDTH = D_HEADS * D_DV
D_CONV = 4
ODD_COLS = 3 * C_WIDTH + 2 * D_QK_WIDTH + 2 * D_WIDTH + 2 * D_HEADS
N_GROUPS = 4
EXPERTS_PER_GROUP = 8
N_EXPERTS = N_GROUPS * EXPERTS_PER_GROUP
LN_EPS = 1e-5
RMS_EPS = 1e-6

LANES = 128
VMEM_LIMIT = 56 * 1024 * 1024
RW_CHUNK = 64
RW_PAIRS = 4
GLA_WIDE_CHUNK = 64
GLA_SAFE_LOG = 60.0
GLA_CHUNK = 16
ML_CHUNK = 256
SEQ_BLOCK = 256
ATT_BLOCK = 128
ATT_GROUP = 8
PROJ_SUB = 128
MOE_BM = 256
TOK_BITS = 15

NN = (((1,), (0,)), ((), ()))
NT = (((1,), (1,)), ((), ()))


def _dot(a, b, dims=NN):
    return lax.dot_general(a.astype(BF16), b.astype(BF16), dims, preferred_element_type=F32)


def _split3(x):
    hi = x.astype(BF16)
    r1 = x - hi.astype(F32)
    mid = r1.astype(BF16)
    lo = (r1 - mid.astype(F32)).astype(BF16)
    return hi, mid, lo


def _dot_exact_lhs(mask, x, dims=NN):
    hi, mid, lo = _split3(x)
    m = mask.astype(BF16)
    d = lambda p: lax.dot_general(m, p, dims, preferred_element_type=F32)
    return d(hi) + d(mid) + d(lo)


def _dot_exact_rhs(x, mask):
    hi, mid, lo = _split3(x)
    m = mask.astype(BF16)
    d = lambda p: lax.dot_general(p, m, NN, preferred_element_type=F32)
    return d(hi) + d(mid) + d(lo)


def _sigmoid(x):
    return 1.0 / (1.0 + jnp.exp(-x))


def _silu(x):
    return x * _sigmoid(x)


def _log_sigmoid(x):
    return jnp.minimum(x, 0.0) - jnp.log1p(jnp.exp(-jnp.abs(x)))


def _params(*sem):
    return pltpu.CompilerParams(dimension_semantics=sem, vmem_limit_bytes=VMEM_LIMIT)


def _mm_kernel(x_ref, w_ref, o_ref):
    o_ref[...] = _dot(x_ref[...], w_ref[...])


def _matmul(x, w, tm=2048, tn=256):
    t, k = x.shape
    n = w.shape[1]
    tm = min(tm, t)
    return pl.pallas_call(
        _mm_kernel,
        name="in_proj",
        grid=(t // tm, n // tn),
        in_specs=[pl.BlockSpec((tm, k), lambda i, j: (i, 0)),
                  pl.BlockSpec((k, tn), lambda i, j: (0, j))],
        out_specs=pl.BlockSpec((tm, tn), lambda i, j: (i, j)),
        out_shape=jax.ShapeDtypeStruct((t, n), F32),
        compiler_params=_params("parallel", "arbitrary"),
    )(x, w)


def _layer_norm(y, w, b):
    mu = jnp.mean(y, -1, keepdims=True)
    yc = y - mu
    var = jnp.mean(yc * yc, -1, keepdims=True)
    return yc * lax.rsqrt(var + LN_EPS) * w + b


def _proj_ln_kernel(ya_ref, yb_ref, wa_ref, wb_ref, x_ref, g_ref, b_ref, rh_ref, rm_ref, rb_ref,
                    o_ref, slab_ref, *, alpha):
    sub = min(PROJ_SUB, o_ref.shape[0])
    for i in range(o_ref.shape[0] // sub):
        rows = slice(i * sub, (i + 1) * sub)
        y = _dot(ya_ref[rows, :], wa_ref[...]) + _dot(yb_ref[rows, :], wb_ref[...]) + alpha * x_ref[rows, :]
        o_ref[rows, :] = _layer_norm(y, g_ref[...], b_ref[...])
    slab_ref[...] = _route(o_ref[...], rh_ref[...], rm_ref[...], rb_ref[...])


def _proj_ln(ya, yb, w_out, x, g, b, router, alpha, tm=512):
    t, d = x.shape
    ka = ya.shape[1]
    kb = yb.shape[1]
    tm = min(tm, t)
    wa = w_out[:ka].astype(BF16)
    wb = w_out[ka:].astype(BF16)
    row = lambda i: (i, 0)
    fix = lambda i: (0, 0)
    return pl.pallas_call(
        functools.partial(_proj_ln_kernel, alpha=alpha),
        name="proj_ln",
        grid=(t // tm,),
        in_specs=[pl.BlockSpec((tm, ka), row), pl.BlockSpec((tm, kb), row),
                  pl.BlockSpec((ka, d), fix), pl.BlockSpec((kb, d), fix),
                  pl.BlockSpec((tm, d), row), pl.BlockSpec((1, d), fix), pl.BlockSpec((1, d), fix),
                  pl.BlockSpec((d, LANES), fix), pl.BlockSpec((d, LANES), fix), pl.BlockSpec((1, LANES), fix)],
        out_specs=[pl.BlockSpec((tm, d), row), pl.BlockSpec((tm, LANES), row)],
        out_shape=[jax.ShapeDtypeStruct((t, d), F32), jax.ShapeDtypeStruct((t, LANES), F32)],
        compiler_params=_params("parallel"),
    )(ya, yb, wa, wb, x, g.reshape(1, d), b.reshape(1, d), *router)


def _route(x, wh, wm, bias):
    xh, xm, _ = _split3(x)
    d = lambda a, w: lax.dot_general(a, w, NN, preferred_element_type=F32)
    logits = d(xh, wh) + d(xh, wm) + d(xm, wh) + bias
    tm = logits.shape[0]
    lane = lax.broadcasted_iota(jnp.int32, (tm, LANES), 1)
    neg = -jnp.inf
    gl = jnp.where(lane < N_GROUPS, logits, neg)
    gmax = jnp.max(gl, -1, keepdims=True)
    g_top = jnp.min(jnp.where(gl == gmax, lane, LANES), -1, keepdims=True)
    p_group = 1.0 / jnp.sum(jnp.exp(gl - gmax), -1, keepdims=True)
    e_lane = lane - N_GROUPS
    in_group = (e_lane >= 0) & (e_lane < N_EXPERTS) & ((e_lane // EXPERTS_PER_GROUP) == g_top)
    el = jnp.where(in_group, logits, neg)
    v1 = jnp.max(el, -1, keepdims=True)
    i1 = jnp.min(jnp.where(el == v1, lane, LANES), -1, keepdims=True)
    el2 = jnp.where(lane == i1, neg, el)
    v2 = jnp.max(el2, -1, keepdims=True)
    i2 = jnp.min(jnp.where(el2 == v2, lane, LANES), -1, keepdims=True)
    e2 = jnp.exp(v2 - v1)
    w1 = 1.0 / (1.0 + e2)
    w2 = e2 / (1.0 + e2)
    return jnp.where(lane == 0, (i1 - N_GROUPS).astype(F32),
                     jnp.where(lane == 1, (i2 - N_GROUPS).astype(F32),
                               jnp.where(lane == 2, p_group * w1,
                                         jnp.where(lane == 3, p_group * w2, 0.0))))


def _router_weights(wg, bg, we, be):
    d = wg.shape[0]
    w = jnp.concatenate([wg, jnp.transpose(we, (1, 0, 2)).reshape(d, N_EXPERTS)], 1)
    w = jnp.pad(w, ((0, 0), (0, LANES - w.shape[1])))
    bias = jnp.pad(jnp.concatenate([bg, be.reshape(-1)]), (0, LANES - N_GROUPS - N_EXPERTS)).reshape(1, LANES)
    wh = w.astype(BF16)
    wm = (w - wh.astype(F32)).astype(BF16)
    return wh, wm, bias


def _ffn_kernel(bexp_ref, nvalid_ref, pk_ref, x_hbm, w1_ref, w3_ref, w2_ref, y_hbm,
                xbuf, ybuf, w1b, w3b, w2b, gsem, ssem, *, bm):
    j = pl.program_id(0)
    nb = pl.num_programs(0)
    nvalid = nvalid_ref[0]
    slot = j % 2

    @pl.when((j < nvalid) & ((j == 0) | (bexp_ref[j] != bexp_ref[jnp.maximum(j - 1, 0)])))
    def _():
        w1b[...] = w1_ref[...].astype(BF16)
        w3b[...] = w3_ref[...].astype(BF16)
        w2b[...] = w2_ref[...].astype(BF16)

    def row_copy_in(blk, s, r):
        src = pk_ref[blk * bm + r] & ((1 << TOK_BITS) - 1)
        return pltpu.make_async_copy(x_hbm.at[pl.ds(src, 1)], xbuf.at[s, pl.ds(r, 1)], gsem.at[s])

    def row_copy_out(blk, s, r):
        dst = lax.shift_right_logical(pk_ref[blk * bm + r], TOK_BITS)
        return pltpu.make_async_copy(ybuf.at[s, pl.ds(r, 1)], y_hbm.at[pl.ds(dst, 1)], ssem.at[s])

    def for_rows(fn):
        def body(r, c):
            fn(r)
            return c
        lax.fori_loop(0, bm, body, 0, unroll=8)

    @pl.when((j == 0) & (nvalid > 0))
    def _():
        for_rows(lambda r: row_copy_in(0, 0, r).start())

    @pl.when(j + 1 < nvalid)
    def _():
        for_rows(lambda r: row_copy_in(j + 1, 1 - slot, r).start())

    def wait_gather(s):
        pltpu.make_async_copy(xbuf.at[s], xbuf.at[s], gsem.at[s]).wait()

    def wait_scatter(s):
        pltpu.make_async_copy(ybuf.at[s], ybuf.at[s], ssem.at[s]).wait()

    @pl.when((j >= 2) & (j - 2 < nvalid))
    def _():
        wait_scatter(slot)

    @pl.when(j < nvalid)
    def _():
        wait_gather(slot)
        xb = xbuf[slot]
        h = _silu(_dot(xb, w1b[...])) * _dot(xb, w3b[...])
        ybuf[slot] = _dot(h, w2b[...])
        for_rows(lambda r: row_copy_out(j, slot, r).start())

    @pl.when(j >= nvalid)
    def _():
        ybuf[slot] = jnp.zeros((bm, ybuf.shape[2]), F32)
        fill = pltpu.make_async_copy(ybuf.at[slot], y_hbm.at[pl.ds(j * bm, bm)], ssem.at[slot])
        fill.start()
        fill.wait()

    @pl.when(j == nb - 1)
    def _():
        @pl.when((j >= 1) & (j - 1 < nvalid))
        def _():
            wait_scatter(1 - slot)

        @pl.when(j < nvalid)
        def _():
            wait_scatter(slot)


def _moe_dispatch(slab, t, bm):
    n = 2 * t
    e = N_EXPERTS
    rows = -(-(n + e * (bm - 1)) // bm) * bm
    ids = slab[:, :2].astype(jnp.int32).reshape(n)
    order = jnp.argsort(ids, stable=True).astype(jnp.int32)
    counts = jnp.sum((ids[:, None] == jnp.arange(e, dtype=jnp.int32)[None, :]).astype(jnp.int32), 0)
    padded = -(-counts // bm) * bm
    pend = jnp.cumsum(padded)
    pstart = pend - padded
    start = jnp.cumsum(counts) - counts
    n_blocks = rows // bm
    bstart = jnp.arange(n_blocks, dtype=jnp.int32) * bm
    block_exp = jnp.minimum(jnp.sum((bstart[:, None] >= pend[None, :]).astype(jnp.int32), 1), e - 1)
    nvalid = (pend[-1] // bm).astype(jnp.int32).reshape(1)
    per_row = lambda tbl: jnp.repeat(tbl[block_exp], bm)
    local = jnp.arange(rows, dtype=jnp.int32) - per_row(pstart)
    valid = local < per_row(counts)
    a = order[jnp.clip(per_row(start) + local, 0, n - 1)]
    pad_rank = jnp.cumsum(1 - valid.astype(jnp.int32)) - 1
    row_src = jnp.where(valid, a // 2, 0)
    row_dst = jnp.where(valid, (a % 2) * t + a // 2, n + pad_rank)
    packed = row_src | (row_dst << TOK_BITS)
    return packed.astype(jnp.int32), block_exp, nvalid, rows


def _moe_ffn(x, slab, w1, w3, w2, layer, bm=MOE_BM):
    t, d = x.shape
    de = w1.shape[3]
    assert t <= (1 << TOK_BITS)
    packed, block_exp, nvalid, rows = _moe_dispatch(slab, t, bm)
    assert rows < (1 << (32 - TOK_BITS))
    n_blocks = rows // bm
    wmap = lambda j, bexp, nv, pk: (layer, bexp[j], 0, 0)
    grid_spec = pltpu.PrefetchScalarGridSpec(
        num_scalar_prefetch=3,
        grid=(n_blocks,),
        in_specs=[pl.BlockSpec(memory_space=pl.ANY),
                  pl.BlockSpec((None, None, d, de), wmap), pl.BlockSpec((None, None, d, de), wmap),
                  pl.BlockSpec((None, None, de, d), wmap)],
        out_specs=pl.BlockSpec(memory_space=pl.ANY),
        scratch_shapes=[pltpu.VMEM((2, bm, d), F32), pltpu.VMEM((2, bm, d), F32),
                        pltpu.VMEM((d, de), BF16), pltpu.VMEM((d, de), BF16), pltpu.VMEM((de, d), BF16),
                        pltpu.SemaphoreType.DMA((2,)), pltpu.SemaphoreType.DMA((2,))],
    )
    return pl.pallas_call(
        functools.partial(_ffn_kernel, bm=bm),
        name="moe_ffn",
        grid_spec=grid_spec,
        out_shape=jax.ShapeDtypeStruct((rows, d), F32),
        compiler_params=_params("arbitrary"),
    )(block_exp, nvalid, packed, x, w1, w3, w2)


def _moe_ln_kernel(x_ref, y0_ref, y1_ref, s_ref, g_ref, b_ref, o_ref, oh_ref, *, alpha):
    s = s_ref[...]
    y = alpha * x_ref[...] + s[:, 2:3] * y0_ref[...] + s[:, 3:4] * y1_ref[...]
    o = _layer_norm(y, g_ref[...], b_ref[...])
    o_ref[...] = o
    oh_ref[...] = o.astype(BF16)


def _moe_ln(x, y, slab, g, b, alpha, tm=512):
    t, d = x.shape
    tm = min(tm, t)
    nt = t // tm
    row = lambda i: (i, 0)
    fix = lambda i: (0, 0)
    return pl.pallas_call(
        functools.partial(_moe_ln_kernel, alpha=alpha),
        name="moe_ln",
        grid=(nt,),
        in_specs=[pl.BlockSpec((tm, d), row), pl.BlockSpec((tm, d), row),
                  pl.BlockSpec((tm, d), lambda i: (i + nt, 0)), pl.BlockSpec((tm, LANES), row),
                  pl.BlockSpec((1, d), fix), pl.BlockSpec((1, d), fix)],
        out_specs=[pl.BlockSpec((tm, d), row), pl.BlockSpec((tm, d), row)],
        out_shape=[jax.ShapeDtypeStruct((t, d), F32), jax.ShapeDtypeStruct((t, d), BF16)],
        compiler_params=_params("parallel"),
    )(x, y, y, slab, g.reshape(1, d), b.reshape(1, d))


def _moe_sublayer(x, slab, w1, w3, w2, layer, g, b, alpha):
    y = _moe_ffn(x, slab, w1, w3, w2, layer)
    return _moe_ln(x, y, slab, g, b, alpha)


def _attn_kernel(q_ref, kp_ref, kc_ref, vp_ref, vc_ref, o_ref, k_all, v_all, q_all, acc_o, acc_m, acc_l, *, rows):
    n = pl.program_id(2)
    k_all[0:rows, :] = kp_ref[...]
    k_all[rows:2 * rows, :] = kc_ref[...]
    v_all[0:rows, :] = vp_ref[...]
    v_all[rows:2 * rows, :] = vc_ref[...]
    q_all[...] = q_ref[...] * (HEAD_DIM ** -0.5)
    nk = ATT_BLOCK
    qi = lax.broadcasted_iota(jnp.int32, (nk, 2 * nk), 0)
    kj = lax.broadcasted_iota(jnp.int32, (nk, 2 * nk), 1)
    band = (kj >= qi) & (kj <= qi + nk)
    lane = lax.broadcasted_iota(jnp.int32, (nk, LANES), 1)
    head1 = lane >= HEAD_DIM

    for p_idx, (window, dil) in enumerate(C_PAIRS):
        assert window // dil == nk
        span = nk * dil
        n_sub = rows // span

        def group(gi, carry, dil=dil, span=span, n_sub=n_sub, first=(p_idx == 0)):
            sel = lambda ref, start: ref[pl.ds(start, nk, stride=dil), :]
            loaded = []
            for i in range(ATT_GROUP):
                u = gi * ATT_GROUP + i
                res = u // n_sub
                sub = u - res * n_sub
                qstart = sub * span + res
                q = sel(q_all, qstart)
                k2 = jnp.concatenate([sel(k_all, rows + qstart - span), sel(k_all, rows + qstart)], 0).astype(BF16)
                v2 = jnp.concatenate([sel(v_all, rows + qstart - span), sel(v_all, rows + qstart)], 0).astype(BF16)
                dst = pl.ds(qstart, nk, stride=dil)
                old = None if first else (acc_o[dst, :], acc_m[dst, :], acc_l[dst, :])
                loaded.append((q, k2, v2, dst, old, sub))
            results = []
            for q, k2, v2, dst, old, sub in loaded:
                ok = band & ((kj >= nk) | (n > 0) | (sub > 0))
                outs = []
                for h1 in (False, True):
                    hm = head1 if h1 else jnp.logical_not(head1)
                    sc = _dot(jnp.where(hm, q, 0.0), k2, NT)
                    sc = jnp.where(ok, sc, -jnp.inf)
                    m = jnp.max(sc, -1, keepdims=True)
                    p = jnp.exp(sc - m)
                    outs.append((_dot(p, v2), m, jnp.sum(p, -1, keepdims=True)))
                o = jnp.where(head1, outs[1][0], outs[0][0])
                m = jnp.where(head1, outs[1][1], outs[0][1])
                l = jnp.where(head1, outs[1][2], outs[0][2])
                if old is not None:
                    o_old, m_old, l_old = old
                    m_new = jnp.maximum(m_old, m)
                    a_old = jnp.exp(m_old - m_new)
                    a_cur = jnp.exp(m - m_new)
                    o = o_old * a_old + o * a_cur
                    l = l_old * a_old + l * a_cur
                    m = m_new
                results.append((dst, o, m, l))
            for dst, o, m, l in results:
                acc_o[dst, :] = o
                acc_m[dst, :] = m
                acc_l[dst, :] = l
            return carry

        assert (dil * n_sub) % ATT_GROUP == 0
        lax.fori_loop(0, dil * n_sub // ATT_GROUP, group, 0)

    o_ref[...] = acc_o[...] / acc_l[...]


def _dilated_attention(p, b, s, qoff, koff, voff):
    rows = max(w for w, _ in C_PAIRS)
    assert s % rows == 0
    nblk = s // rows
    cur = lambda off: pl.BlockSpec((rows, LANES), lambda bi, hp, n: (bi * nblk + n, off + hp))
    prev = lambda off: pl.BlockSpec((rows, LANES), lambda bi, hp, n: (bi * nblk + jnp.maximum(n - 1, 0), off + hp))
    scratch = [pltpu.VMEM((2 * rows, LANES), F32), pltpu.VMEM((2 * rows, LANES), F32)] + \
              [pltpu.VMEM((rows, LANES), F32)] * 4
    return pl.pallas_call(
        functools.partial(_attn_kernel, rows=rows),
        name="dil_attn",
        grid=(b, C_WIDTH // LANES, nblk),
        in_specs=[cur(qoff), prev(koff), cur(koff), prev(voff), cur(voff)],
        out_specs=pl.BlockSpec((rows, LANES), lambda bi, hp, n: (bi * nblk + n, hp)),
        out_shape=jax.ShapeDtypeStruct((b * s, C_WIDTH), F32),
        scratch_shapes=scratch,
        compiler_params=_params("parallel", "parallel", "arbitrary"),
    )(p, p, p, p, p)


def _causal_conv(x_ref, pad_ref, w_ref, b_ref, first, blk):
    @pl.when(first)
    def _():
        pad_ref[0:8, :] = jnp.zeros((8, pad_ref.shape[1]), F32)
    pad_ref[8:8 + blk, :] = x_ref[...]
    w = w_ref[...]
    acc = b_ref[...]
    for j in range(D_CONV):
        acc = acc + w[j:j + 1, :] * pad_ref[pl.ds(8 - (D_CONV - 1) + j, blk), :]
    pad_ref[0:8, :] = pad_ref[blk:blk + 8, :]
    return acc


def _mlstm_kernel(q_ref, k_ref, v_ref, og_ref, gate_ref, cwq_ref, cwk_ref, cbq_ref, cbk_ref, gb_ref,
                  o_ref, qpad, kpad, c_state, n_state, m_state, *, blk, chunk):
    first = pl.program_id(1) == 0

    @pl.when(first)
    def _():
        c_state[...] = jnp.zeros_like(c_state)
        n_state[...] = jnp.zeros_like(n_state)
        m_state[...] = jnp.zeros_like(m_state)

    qc = _silu(_causal_conv(q_ref, qpad, cwq_ref, cbq_ref, first, blk))
    kc = _silu(_causal_conv(k_ref, kpad, cwk_ref, cbk_ref, first, blk)) * (D_DQK ** -0.5)
    gates = gate_ref[...] + gb_ref[...]
    logf = _log_sigmoid(gates)
    lane = lax.broadcasted_iota(jnp.int32, (chunk, LANES), 1)
    ti = lax.broadcasted_iota(jnp.int32, (chunk, chunk), 0)
    si = lax.broadcasted_iota(jnp.int32, (chunk, chunk), 1)
    causal = si <= ti
    tri = causal.astype(F32)
    pick = (lax.broadcasted_iota(jnp.int32, (2 * D_HEADS, LANES), 0)
            == lax.broadcasted_iota(jnp.int32, (2 * D_HEADS, LANES), 1)).astype(F32)

    c_mat = [c_state[h] for h in range(D_HEADS)]
    n_vec = [n_state[h:h + 1, :] for h in range(D_HEADS)]
    m_prev = [m_state[h:h + 1, 0:1] for h in range(D_HEADS)]
    for c in range(blk // chunk):
        rows = slice(c * chunk, (c + 1) * chunk)
        gi = gates[rows]
        fcum_all = _dot_exact_lhs(tri, logf[rows])
        f_rows = _dot_exact_lhs(pick, fcum_all, NT)
        i_rows = _dot_exact_lhs(pick, gi, NT)
        for h in range(D_HEADS):
            sel_i = lane == h
            sel_f = lane == D_HEADS + h
            qs = slice(h * D_DQK, (h + 1) * D_DQK)
            vs = slice(h * D_DV, (h + 1) * D_DV)
            q_c, k_c, v_c = qc[rows, qs], kc[rows, qs], v_ref[rows, vs]
            f_col = jnp.sum(jnp.where(sel_f, fcum_all, 0.0), -1, keepdims=True)
            i_col = jnp.sum(jnp.where(sel_i, gi, 0.0), -1, keepdims=True)
            f_row = f_rows[D_HEADS + h:D_HEADS + h + 1, :]
            i_row = i_rows[h:h + 1, :]
            log_d = jnp.where(causal, f_col - f_row + i_row, -jnp.inf)
            log_inter = f_col + m_prev[h]
            m_t = jnp.maximum(jnp.max(log_d, -1, keepdims=True), log_inter)
            w_inter = jnp.exp(log_inter - m_t)
            scores = _dot(q_c, k_c, NT) * jnp.exp(log_d - m_t)
            num = w_inter * _dot(q_c, c_mat[h]) + _dot(scores, v_c)
            den = w_inter * jnp.sum(q_c * n_vec[h], -1, keepdims=True) + jnp.sum(scores, -1, keepdims=True)
            out = num / jnp.maximum(jnp.abs(den), jnp.exp(-m_t))
            o_ref[rows, vs] = _sigmoid(og_ref[rows, vs]) * out
            f_tot = f_col[chunk - 1:chunk]
            log_w = f_tot - f_col + i_col
            m_new = jnp.maximum(f_tot + m_prev[h], jnp.max(log_w, 0, keepdims=True))
            kw = k_c * jnp.exp(log_w - m_new)
            w_old = jnp.exp(f_tot + m_prev[h] - m_new)
            c_mat[h] = w_old * c_mat[h] + _dot(kw.T, v_c)
            n_vec[h] = w_old * n_vec[h] + jnp.sum(kw, 0, keepdims=True)
            m_prev[h] = m_new
    for h in range(D_HEADS):
        c_state[h] = c_mat[h]
        n_state[h:h + 1, :] = n_vec[h]
        m_state[h:h + 1, :] = jnp.broadcast_to(m_prev[h], (1, LANES))


def _mlstm(p, b, s, off, conv_w, conv_b, b_i, b_f, blk=SEQ_BLOCK, chunk=ML_CHUNK):
    blk = min(blk, s)
    nblk = s // blk
    hq = D_QK_WIDTH // LANES
    hv = D_WIDTH // LANES
    assert off % hq == 0 and (off + 2 * hq) % hv == 0
    rowblk = lambda bi, n: bi * nblk + n
    qk_spec = lambda o: pl.BlockSpec((blk, D_QK_WIDTH), lambda bi, n: (rowblk(bi, n), off // hq + o))
    v_spec = lambda o: pl.BlockSpec((blk, D_WIDTH), lambda bi, n: (rowblk(bi, n), (off + 2 * hq) // hv + o))
    par_spec = lambda rows, o: pl.BlockSpec((rows, D_QK_WIDTH), lambda bi, n: (0, o))
    gate_off = off + 2 * hq + 2 * hv
    gbias = jnp.pad(jnp.concatenate([b_i, b_f]), (0, LANES - 2 * D_HEADS)).reshape(1, LANES)
    cb = conv_b.reshape(1, -1)
    return pl.pallas_call(
        functools.partial(_mlstm_kernel, blk=blk, chunk=chunk),
        name="mlstm",
        grid=(b, nblk),
        in_specs=[qk_spec(0), qk_spec(1), v_spec(0), v_spec(1),
                  pl.BlockSpec((blk, LANES), lambda bi, n: (rowblk(bi, n), gate_off)),
                  par_spec(D_CONV, 0), par_spec(D_CONV, 1), par_spec(1, 0), par_spec(1, 1),
                  pl.BlockSpec((1, LANES), lambda bi, n: (0, 0))],
        out_specs=pl.BlockSpec((blk, D_WIDTH), lambda bi, n: (rowblk(bi, n), 0)),
        out_shape=jax.ShapeDtypeStruct((b * s, D_WIDTH), F32),
        scratch_shapes=[pltpu.VMEM((blk + 8, D_QK_WIDTH), F32), pltpu.VMEM((blk + 8, D_QK_WIDTH), F32),
                        pltpu.VMEM((D_HEADS, D_DQK, D_DV), F32), pltpu.VMEM((8, LANES), F32),
                        pltpu.VMEM((8, LANES), F32)],
        compiler_params=_params("parallel", "arbitrary"),
    )(p, p, p, p, p, conv_w, conv_w, cb, cb, gbias)


TN = (((0,), (0,)), ((), ()))


def _chunk_masks(blk, chunk):
    t = lax.broadcasted_iota(jnp.int32, (blk, blk), 0)
    s = lax.broadcasted_iota(jnp.int32, (blk, blk), 1)
    same = (t // chunk) == (s // chunk)
    return same, same & (s <= t), same & (s < t)


def _hgrn2_kernel(q_ref, f_ref, i_ref, g_ref, lb_ref, nw_ref, o_ref, s_state, o_scr, *, blk, chunk, wide):
    @pl.when(pl.program_id(2) == 0)
    def _():
        s_state[...] = jnp.zeros_like(s_state)

    lb = lb_ref[...]
    f = f_ref[...]
    q = _silu(q_ref[...])
    v = i_ref[...]
    a = jnp.log(lb)
    bb = jnp.log1p(-lb) + _log_sigmoid(f)
    log_f = jnp.maximum(a, bb) + jnp.log1p(jnp.exp(-jnp.abs(a - bb)))
    k = (1.0 - lb) * _sigmoid(-f)

    _, tri_w, _ = _chunk_masks(blk, wide)
    bw = _dot_exact_lhs(tri_w.astype(F32), log_f)
    wchunks = [bw[c * wide:(c + 1) * wide] for c in range(blk // wide)]
    mids = [bc[wide // 2 - 1:wide // 2] for bc in wchunks]
    spread = [jnp.max(jnp.abs(bc - m)) for bc, m in zip(wchunks, mids)]
    safe = functools.reduce(jnp.maximum, spread) < GLA_SAFE_LOG

    @pl.when(safe)
    def _():
        ti = lax.broadcasted_iota(jnp.int32, (wide, wide), 0)
        si = lax.broadcasted_iota(jnp.int32, (wide, wide), 1)
        st = s_state[...]
        for c, (bc, m) in enumerate(zip(wchunks, mids)):
            rows = slice(c * wide, (c + 1) * wide)
            q_c, k_c, v_c = q[rows], k[rows], v[rows]
            total = bc[wide - 1:wide]
            att = jnp.where(si <= ti, _dot(q_c * jnp.exp(bc - m), k_c * jnp.exp(m - bc), NT), 0.0)
            o_scr[rows, :] = _dot(att, v_c) + _dot(q_c * jnp.exp(bc), st, NT)
            st = st * jnp.exp(total) + _dot(v_c, k_c * jnp.exp(total - bc), TN)
        s_state[...] = st

    @pl.when(jnp.logical_not(safe))
    def _():
        same, tri_incl, _ = _chunk_masks(blk, chunk)
        bcum = _dot_exact_lhs(tri_incl.astype(F32), log_f)
        blast = _dot_exact_lhs(same.astype(F32), log_f)
        qe = q * jnp.exp(bcum)
        kd = k * jnp.exp(blast - bcum)
        decay = jnp.exp(blast)
        pos = lax.broadcasted_iota(jnp.int32, (chunk, LANES), 0)
        st = s_state[...]
        for c in range(blk // chunk):
            rows = slice(c * chunk, (c + 1) * chunk)
            q_c, k_c, v_c, b_c = q[rows], k[rows], v[rows], bcum[rows]
            o_c = _dot(qe[rows], st, NT) + jnp.sum(q_c * k_c, -1, keepdims=True) * v_c
            for dist in range(1, chunk):
                e = jnp.exp(jnp.where(pos >= dist, b_c - pltpu.roll(b_c, dist, 0), -jnp.inf))
                w = jnp.sum(q_c * pltpu.roll(k_c, dist, 0) * e, -1, keepdims=True)
                o_c = o_c + w * pltpu.roll(v_c, dist, 0)
            o_scr[rows, :] = o_c
            st = st * decay[c * chunk:c * chunk + 1] + _dot(v_c, kd[rows], TN)
        s_state[...] = st

    o = o_scr[...]
    o = o * lax.rsqrt(jnp.mean(o * o, -1, keepdims=True) + RMS_EPS) * nw_ref[...]
    o_ref[...] = o * _silu(g_ref[...])


def _hgrn2(p, b, s, off, lb, norm_w, blk=SEQ_BLOCK, chunk=GLA_CHUNK):
    blk = min(blk, s)
    nblk = s // blk
    spec = lambda o: pl.BlockSpec((blk, LANES), lambda bi, h, n: (bi * nblk + n, off + o * B_HEADS + h))
    return pl.pallas_call(
        functools.partial(_hgrn2_kernel, blk=blk, chunk=chunk, wide=min(GLA_WIDE_CHUNK, blk)),
        name="hgrn2",
        grid=(b, B_HEADS, nblk),
        in_specs=[spec(0), spec(1), spec(2), spec(3),
                  pl.BlockSpec((1, LANES), lambda bi, h, n: (0, h)),
                  pl.BlockSpec((1, LANES), lambda bi, h, n: (0, 0))],
        out_specs=pl.BlockSpec((blk, LANES), lambda bi, h, n: (bi * nblk + n, h)),
        out_shape=jax.ShapeDtypeStruct((b * s, B_WIDTH), F32),
        scratch_shapes=[pltpu.VMEM((B_DV, B_DK), F32), pltpu.VMEM((blk, LANES), F32)],
        compiler_params=_params("parallel", "parallel", "arbitrary"),
    )(p, p, p, p, lb.reshape(1, -1), norm_w.reshape(1, -1))


def _rwkv7_kernel(r_ref, k_ref, v_ref, wa_ref, xg_ref, mur_ref, muk_ref, muv_ref, muwa_ref, mug_ref,
                  w0_ref, a0_ref, kk_ref, ka_ref, rk_ref, lnw_ref, lnb_ref, w2_ref, a2_ref, g2_ref,
                  o_ref, tail, s_state, *, blk, chunk, npp):
    @pl.when(pl.program_id(2) == 0)
    def _():
        tail[...] = jnp.zeros_like(tail)
        s_state[...] = jnp.zeros_like(s_state)

    row0 = lax.broadcasted_iota(jnp.int32, (blk, LANES), 0) == 0

    def shift_lerp(idx, x, mu):
        prev = tail[0:1, idx * LANES:(idx + 1) * LANES]
        shifted = jnp.where(row0, prev, pltpu.roll(x, 1, 0))
        tail[0:1, idx * LANES:(idx + 1) * LANES] = x[blk - 1:blk]
        return x + (shifted - x) * mu

    wa = shift_lerp(3 * npp, wa_ref[...], muwa_ref[...])
    xg = shift_lerp(3 * npp + 1, xg_ref[...], mug_ref[...])
    tanh_wa = jnp.tanh(wa)
    sig_xg = _sigmoid(xg)

    lane = lax.broadcasted_iota(jnp.int32, (LANES, LANES), 1)
    sub = lax.broadcasted_iota(jnp.int32, (LANES, LANES), 0)
    head_bd = (lane // HEAD_DIM) == (sub // HEAD_DIM)
    first_head = lax.broadcasted_iota(jnp.int32, (blk, LANES), 1) < HEAD_DIM

    def head_sum(x):
        s0 = jnp.sum(jnp.where(first_head, x, 0.0), -1, keepdims=True)
        s1 = jnp.sum(jnp.where(first_head, 0.0, x), -1, keepdims=True)
        return jnp.where(first_head, s0, s1)

    _, tri_incl, _ = _chunk_masks(blk, chunk)
    tri_incl = tri_incl.astype(F32)
    c2 = 2 * chunk
    lane_c = lax.broadcasted_iota(jnp.int32, (chunk, LANES), 1)
    h1 = lane_c >= HEAD_DIM
    h0 = jnp.logical_not(h1)
    stack = lambda x: jnp.concatenate([jnp.where(h0, x, 0.0), jnp.where(h1, x, 0.0)], 0)
    ri = lax.broadcasted_iota(jnp.int32, (2 * c2, 2 * c2), 0)
    ci = lax.broadcasted_iota(jnp.int32, (2 * c2, 2 * c2), 1)
    amask = ((ci % chunk) < (ri % chunk)) | (((ci % chunk) == (ri % chunk)) & (ri >= c2))
    eye = (lax.broadcasted_iota(jnp.int32, (c2, c2), 0) == lax.broadcasted_iota(jnp.int32, (c2, c2), 1)).astype(F32)
    nchunk = blk // chunk

    pairs = []
    for pi in range(npp):
        ls = slice(pi * LANES, (pi + 1) * LANES)
        r = shift_lerp(3 * pi, r_ref[:, ls], mur_ref[:, ls])
        k = shift_lerp(3 * pi + 1, k_ref[:, ls], muk_ref[:, ls])
        v = shift_lerp(3 * pi + 2, v_ref[:, ls], muv_ref[:, ls])
        pre_w = -(w0_ref[:, ls] + _dot(tanh_wa, w2_ref[:, ls]))
        w_log = -(jnp.maximum(pre_w, 0.0) + jnp.log1p(jnp.exp(-jnp.abs(pre_w)))) - 0.5
        lw = -jnp.exp(w_log)
        a = _sigmoid(a0_ref[:, ls] + _dot(wa, a2_ref[:, ls]))
        g = _dot(sig_xg, g2_ref[:, ls])
        kk = k * kk_ref[:, ls]
        kk = kk / jnp.maximum(jnp.sqrt(head_sum(kk * kk)), 1e-12)
        k = k * (1.0 + (a - 1.0) * ka_ref[:, ls])
        bonus = head_sum(r * k * rk_ref[:, ls]) * v
        av = -kk
        bv = kk * a
        lcum = _dot_exact_lhs(tri_incl, lw)
        lsum = jnp.concatenate([jnp.broadcast_to(lcum[(c + 1) * chunk - 1:(c + 1) * chunk], (chunk, LANES))
                                for c in range(nchunk)], 0)
        a_t = av * jnp.exp(lcum - lw)
        r_t = r * jnp.exp(lcum)
        inv = jnp.exp(-lcum)
        b_t = bv * inv
        k_t = k * inv
        to_end = jnp.exp(lsum - lcum)
        ars, amats = [], []
        for c in range(nchunk):
            rows = slice(c * chunk, (c + 1) * chunk)
            ar = jnp.concatenate([stack(a_t[rows]), stack(r_t[rows])], 0)
            bk = jnp.concatenate([stack(b_t[rows]), stack(k_t[rows])], 0)
            ars.append(ar)
            amats.append(jnp.where(amask, _dot(ar, bk, NT), 0.0))
        pairs.append(dict(v=v, g=g, bonus=bonus, b_c=bv * to_end, k_c=k * to_end, gam=jnp.exp(lsum),
                          ars=ars, amats=amats))

    keys = [(pi, c) for pi in range(npp) for c in range(nchunk)]
    xs = {key: pairs[key[0]]["amats"][key[1]][:c2, :c2] for key in keys}
    t_inv = {key: eye + xs[key] for key in keys}
    for _ in range(chunk.bit_length() - 2):
        for key in keys:
            xs[key] = _dot(xs[key], xs[key])
        for key in keys:
            t_inv[key] = t_inv[key] + _dot(t_inv[key], xs[key])

    sts = [s_state[pi] for pi in range(npp)]
    ys = [[] for _ in range(npp)]
    for c in range(nchunk):
        rows = slice(c * chunk, (c + 1) * chunk)
        for pi in range(npp):
            pr = pairs[pi]
            ar, amat = pr["ars"][c], pr["amats"][c]
            v_c = pr["v"][rows]
            v2 = jnp.concatenate([v_c, v_c], 0)
            g1 = _dot(ar, sts[pi], NT)
            u2 = _dot(t_inv[(pi, c)], g1[:c2] + _dot(amat[:c2, c2:], v2))
            y2 = g1[c2:] + _dot(amat[c2:], jnp.concatenate([u2, v2], 0))
            ys[pi].append(jnp.where(h1, y2[chunk:], y2[:chunk]))
            u = jnp.where(h1, u2[chunk:], u2[:chunk])
            upd = _dot(jnp.concatenate([u, v_c], 0), jnp.concatenate([pr["b_c"][rows], pr["k_c"][rows]], 0), TN)
            sts[pi] = jnp.where(head_bd, sts[pi] * pr["gam"][c * chunk:c * chunk + 1] + upd, 0.0)

    for pi in range(npp):
        ls = slice(pi * LANES, (pi + 1) * LANES)
        s_state[pi] = sts[pi]
        y = jnp.concatenate(ys[pi], 0)
        mu = head_sum(y) * (1.0 / HEAD_DIM)
        yc = y - mu
        var = head_sum(yc * yc) * (1.0 / HEAD_DIM)
        y = yc * lax.rsqrt(var + A_GN_EPS) * lnw_ref[:, ls] + lnb_ref[:, ls] + pairs[pi]["bonus"]
        o_ref[:, ls] = y * pairs[pi]["g"]


def _rwkv7(p, b, s, mu, w0, w2, a0, a2, g2, k_k, k_a, r_k, lnx_w, lnx_b, blk=SEQ_BLOCK, chunk=RW_CHUNK):
    blk = min(blk, s)
    nblk = s // blk
    npp = RW_PAIRS
    wide = npp * LANES
    ngrp = A_WIDTH // wide
    lora = 3 * A_WIDTH // LANES
    act = lambda o: pl.BlockSpec((blk, wide), lambda bi, hp, n: (bi * nblk + n, o * ngrp + hp))
    act_fix = lambda o: pl.BlockSpec((blk, LANES), lambda bi, hp, n: (bi * nblk + n, o))
    vec = lambda o: pl.BlockSpec((1, wide), lambda bi, hp, n: (0, o * ngrp + hp))
    vec_fix = lambda o: pl.BlockSpec((1, LANES), lambda bi, hp, n: (0, o))
    mat = pl.BlockSpec((LANES, wide), lambda bi, hp, n: (0, hp))
    row = lambda t: t.reshape(1, -1)
    w2p = jnp.concatenate([w2, jnp.zeros_like(a2)], 0)
    a2p = jnp.concatenate([jnp.zeros_like(w2), a2], 0)
    mu2 = row(mu)
    return pl.pallas_call(
        functools.partial(_rwkv7_kernel, blk=blk, chunk=chunk, npp=npp),
        name="rwkv7",
        grid=(b, ngrp, nblk),
        in_specs=[act(0), act(1), act(2), act_fix(lora), act_fix(lora + 1),
                  vec(0), vec(1), vec(2), vec_fix(lora), vec_fix(lora + 1),
                  vec(0), vec(0), vec(0), vec(0), vec(0), vec(0), vec(0), mat, mat, mat],
        out_specs=pl.BlockSpec((blk, wide), lambda bi, hp, n: (bi * nblk + n, hp)),
        out_shape=jax.ShapeDtypeStruct((b * s, A_WIDTH), F32),
        scratch_shapes=[pltpu.VMEM((8, (3 * npp + 2) * LANES), F32), pltpu.VMEM((npp, LANES, LANES), F32)],
        compiler_params=_params("parallel", "parallel", "arbitrary"),
    )(p, p, p, p, p, mu2, mu2, mu2, mu2, mu2,
      row(w0), row(a0), row(k_k), row(k_a), row(r_k), row(lnx_w), row(lnx_b), w2p, a2p, g2)


def kernel(x, ev_w_in, ev_w_out, rwkv_mu, rwkv_w0, rwkv_w2, rwkv_a0, rwkv_a2, rwkv_g2, rwkv_kk, rwkv_ka, rwkv_rk,
           rwkv_lnx_w, rwkv_lnx_b, hgrn_lb, hgrn_norm_w, od_w_in, od_w_out, mlstm_conv_w, mlstm_conv_b,
           mlstm_b_i, mlstm_b_f, ln_w, ln_b, moe_wg, moe_bg, moe_we, moe_be, moe_w1, moe_w3, moe_w2):
    b, s, d = x.shape
    depth = ln_w.shape[0]
    alpha = (2 * depth) ** 0.25
    lb_all = jnp.cumsum(jax.nn.softmax(hgrn_lb.astype(F32), axis=0), axis=0)
    lb_all = lb_all - lb_all[:1]
    tn = 2 * LANES
    xf = x.reshape(b * s, d)
    xh = xf
    for layer in range(depth):
        j = layer // 2
        if layer % 2 == 0:
            p = _matmul(xh, ev_w_in[j].astype(BF16), tn=tn)
            ya = _rwkv7(p, b, s, rwkv_mu[j], rwkv_w0[j], rwkv_w2[j], rwkv_a0[j], rwkv_a2[j], rwkv_g2[j],
                        rwkv_kk[j], rwkv_ka[j], rwkv_rk[j], rwkv_lnx_w[j], rwkv_lnx_b[j])
            yb = _hgrn2(p, b, s, A_COLS // LANES, lb_all[j], hgrn_norm_w[j])
            w_out = ev_w_out[j]
        else:
            w_in = od_w_in[j]
            w_in = jnp.pad(w_in, ((0, 0), (0, -w_in.shape[1] % tn))).astype(BF16)
            p = _matmul(xh, w_in, tn=tn)
            nq = C_WIDTH // LANES
            ya = _dilated_attention(p, b, s, 0, nq, 2 * nq)
            yb = _mlstm(p, b, s, 3 * nq, mlstm_conv_w[j], mlstm_conv_b[j], mlstm_b_i[j], mlstm_b_f[j])
            w_out = od_w_out[j]
        router = _router_weights(moe_wg[layer], moe_bg[layer], moe_we[layer], moe_be[layer])
        xf, slab = _proj_ln(ya, yb, w_out, xf, ln_w[layer, 0], ln_b[layer, 0], router, alpha)
        xf, xh = _moe_sublayer(xf, slab, moe_w1, moe_w3, moe_w2, layer, ln_w[layer, 1], ln_b[layer, 1], alpha)
    return xf.reshape(b, s, d)
```

```python
import functools

import jax
import jax.numpy as jnp
from jax import lax
from jax.experimental import pallas as pl
from jax.experimental.pallas import tpu as pltpu

F32 = jnp.float32
BF16 = jnp.bfloat16

HEAD_DIM = 64
A_HEADS = 16
A_WIDTH = A_HEADS * HEAD_DIM
A_LORA_W = 64
A_LORA_A = 64
A_LORA_G = 128
A_COLS = 3 * A_WIDTH + A_LORA_W + A_LORA_A + A_LORA_G
A_GN_EPS = 64e-5
B_HEADS = 8
B_DK = 128
B_DV = 128
B_WIDTH = B_HEADS * B_DV
C_HEADS = 16
C_WIDTH = C_HEADS * HEAD_DIM
C_PAIRS = ((128, 1), (512, 4), (2048, 16))
D_HEADS = 4
D_DQK = 128
D_DV = 256
D_QK_WIDTH = D_HEADS * D_DQK
D_WIDTH = D_HEADS * D_DV
D_CONV = 4
ODD_COLS = 3 * C_WIDTH + 2 * D_QK_WIDTH + 2 * D_WIDTH + 2 * D_HEADS
N_GROUPS = 4
EXPERTS_PER_GROUP = 8
N_EXPERTS = N_GROUPS * EXPERTS_PER_GROUP
LN_EPS = 1e-5
RMS_EPS = 1e-6

LANES = 128
VMEM_LIMIT = 56 * 1024 * 1024
RW_CHUNK = 64
RW_PAIRS = 4
GLA_WIDE_CHUNK = 64
GLA_SAFE_LOG = 60.0
GLA_CHUNK = 16
ML_CHUNK = 256
SEQ_BLOCK = 256
ATT_BLOCK = 128
ATT_GROUP = 8
IN_PROJ_TILE_BYTES = 16 * 1024 * 1024
PROJ_SUB = 128
MOE_BM = 256
TOK_BITS = 15

NN = (((1,), (0,)), ((), ()))
NT = (((1,), (1,)), ((), ()))


def _dot(a, b, dims=NN):
    return lax.dot_general(a.astype(BF16), b.astype(BF16), dims, preferred_element_type=F32)


def _split3(x):
    hi = x.astype(BF16)
    r1 = x - hi.astype(F32)
    mid = r1.astype(BF16)
    lo = (r1 - mid.astype(F32)).astype(BF16)
    return hi, mid, lo


def _dot_exact_lhs(mask, x, dims=NN):
    hi, mid, lo = _split3(x)
    m = mask.astype(BF16)
    d = lambda p: lax.dot_general(m, p, dims, preferred_element_type=F32)
    return d(hi) + d(mid) + d(lo)


def _dot_exact_rhs(x, mask):
    hi, mid, lo = _split3(x)
    m = mask.astype(BF16)
    d = lambda p: lax.dot_general(p, m, NN, preferred_element_type=F32)
    return d(hi) + d(mid) + d(lo)


def _sigmoid(x):
    return 1.0 / (1.0 + jnp.exp(-x))


def _silu(x):
    return x * _sigmoid(x)


def _log_sigmoid(x):
    return jnp.minimum(x, 0.0) - jnp.log1p(jnp.exp(-jnp.abs(x)))


def _params(*sem):
    return pltpu.CompilerParams(dimension_semantics=sem, vmem_limit_bytes=VMEM_LIMIT)


def _mm_kernel(x_ref, w_ref, o_ref):
    o_ref[...] = _dot(x_ref[...], w_ref[...])


def _matmul(x, w, tn=256):
    t, k = x.shape
    n = w.shape[1]
    tm = min(IN_PROJ_TILE_BYTES // (k * x.dtype.itemsize), t)
    return pl.pallas_call(
        _mm_kernel,
        name="in_proj",
        grid=(t // tm, n // tn),
        in_specs=[pl.BlockSpec((tm, k), lambda i, j: (i, 0)),
                  pl.BlockSpec((k, tn), lambda i, j: (0, j))],
        out_specs=pl.BlockSpec((tm, tn), lambda i, j: (i, j)),
        out_shape=jax.ShapeDtypeStruct((t, n), F32),
        compiler_params=_params("parallel", "arbitrary"),
    )(x, w)


def _layer_norm(y, w, b):
    mu = jnp.mean(y, -1, keepdims=True)
    yc = y - mu
    var = jnp.mean(yc * yc, -1, keepdims=True)
    return yc * lax.rsqrt(var + LN_EPS) * w + b


def _proj_ln_kernel(ya_ref, yb_ref, wa_ref, wb_ref, x_ref, g_ref, b_ref, rh_ref, rm_ref, rb_ref,
                    o_ref, slab_ref, *, alpha):
    sub = min(PROJ_SUB, o_ref.shape[0])
    for i in range(o_ref.shape[0] // sub):
        rows = slice(i * sub, (i + 1) * sub)
        y = _dot(ya_ref[rows, :], wa_ref[...]) + _dot(yb_ref[rows, :], wb_ref[...]) + alpha * x_ref[rows, :]
        o_ref[rows, :] = _layer_norm(y, g_ref[...], b_ref[...])
    slab_ref[...] = _route(o_ref[...], rh_ref[...], rm_ref[...], rb_ref[...])


def _proj_ln(ya, yb, w_out, x, g, b, router, alpha, tm=512):
    t, d = x.shape
    ka = ya.shape[1]
    kb = yb.shape[1]
    tm = min(tm, t)
    wa = w_out[:ka].astype(BF16)
    wb = w_out[ka:].astype(BF16)
    row = lambda i: (i, 0)
    fix = lambda i: (0, 0)
    return pl.pallas_call(
        functools.partial(_proj_ln_kernel, alpha=alpha),
        name="proj_ln",
        grid=(t // tm,),
        in_specs=[pl.BlockSpec((tm, ka), row), pl.BlockSpec((tm, kb), row),
                  pl.BlockSpec((ka, d), fix), pl.BlockSpec((kb, d), fix),
                  pl.BlockSpec((tm, d), row), pl.BlockSpec((1, d), fix), pl.BlockSpec((1, d), fix),
                  pl.BlockSpec((d, LANES), fix), pl.BlockSpec((d, LANES), fix), pl.BlockSpec((1, LANES), fix)],
        out_specs=[pl.BlockSpec((tm, d), row), pl.BlockSpec((tm, LANES), row)],
        out_shape=[jax.ShapeDtypeStruct((t, d), F32), jax.ShapeDtypeStruct((t, LANES), F32)],
        compiler_params=_params("parallel"),
    )(ya, yb, wa, wb, x, g.reshape(1, d), b.reshape(1, d), *router)


def _route(x, wh, wm, bias):
    xh, xm, _ = _split3(x)
    d = lambda a, w: lax.dot_general(a, w, NN, preferred_element_type=F32)
    logits = d(xh, wh) + d(xh, wm) + d(xm, wh) + bias
    tm = logits.shape[0]
    lane = lax.broadcasted_iota(jnp.int32, (tm, LANES), 1)
    neg = -jnp.inf
    gl = jnp.where(lane < N_GROUPS, logits, neg)
    gmax = jnp.max(gl, -1, keepdims=True)
    g_top = jnp.min(jnp.where(gl == gmax, lane, LANES), -1, keepdims=True)
    p_group = 1.0 / jnp.sum(jnp.exp(gl - gmax), -1, keepdims=True)
    e_lane = lane - N_GROUPS
    in_group = (e_lane >= 0) & (e_lane < N_EXPERTS) & ((e_lane // EXPERTS_PER_GROUP) == g_top)
    el = jnp.where(in_group, logits, neg)
    v1 = jnp.max(el, -1, keepdims=True)
    i1 = jnp.min(jnp.where(el == v1, lane, LANES), -1, keepdims=True)
    el2 = jnp.where(lane == i1, neg, el)
    v2 = jnp.max(el2, -1, keepdims=True)
    i2 = jnp.min(jnp.where(el2 == v2, lane, LANES), -1, keepdims=True)
    e2 = jnp.exp(v2 - v1)
    w1 = 1.0 / (1.0 + e2)
    w2 = e2 / (1.0 + e2)
    return jnp.where(lane == 0, (i1 - N_GROUPS).astype(F32),
                     jnp.where(lane == 1, (i2 - N_GROUPS).astype(F32),
                               jnp.where(lane == 2, p_group * w1,
                                         jnp.where(lane == 3, p_group * w2, 0.0))))


def _router_weights(wg, bg, we, be):
    d = wg.shape[0]
    w = jnp.concatenate([wg, jnp.transpose(we, (1, 0, 2)).reshape(d, N_EXPERTS)], 1)
    w = jnp.pad(w, ((0, 0), (0, LANES - w.shape[1])))
    bias = jnp.pad(jnp.concatenate([bg, be.reshape(-1)]), (0, LANES - N_GROUPS - N_EXPERTS)).reshape(1, LANES)
    wh = w.astype(BF16)
    wm = (w - wh.astype(F32)).astype(BF16)
    return wh, wm, bias


def _ffn_kernel(bexp_ref, nvalid_ref, pk_ref, x_hbm, w1_ref, w3_ref, w2_ref, y_hbm,
                xbuf, ybuf, w1b, w3b, w2b, gsem, ssem, *, bm):
    j = pl.program_id(0)
    nb = pl.num_programs(0)
    nvalid = nvalid_ref[0]
    slot = j % 2

    @pl.when((j < nvalid) & ((j == 0) | (bexp_ref[j] != bexp_ref[jnp.maximum(j - 1, 0)])))
    def _():
        w1b[...] = w1_ref[...].astype(BF16)
        w3b[...] = w3_ref[...].astype(BF16)
        w2b[...] = w2_ref[...].astype(BF16)

    def row_copy_in(blk, s, r):
        src = pk_ref[blk * bm + r] & ((1 << TOK_BITS) - 1)
        return pltpu.make_async_copy(x_hbm.at[pl.ds(src, 1)], xbuf.at[s, pl.ds(r, 1)], gsem.at[s])

    def row_copy_out(blk, s, r):
        dst = lax.shift_right_logical(pk_ref[blk * bm + r], TOK_BITS)
        return pltpu.make_async_copy(ybuf.at[s, pl.ds(r, 1)], y_hbm.at[pl.ds(dst, 1)], ssem.at[s])

    def for_rows(fn):
        def body(r, c):
            fn(r)
            return c
        lax.fori_loop(0, bm, body, 0, unroll=8)

    @pl.when((j == 0) & (nvalid > 0))
    def _():
        for_rows(lambda r: row_copy_in(0, 0, r).start())

    @pl.when(j + 1 < nvalid)
    def _():
        for_rows(lambda r: row_copy_in(j + 1, 1 - slot, r).start())

    def wait_gather(s):
        pltpu.make_async_copy(xbuf.at[s], xbuf.at[s], gsem.at[s]).wait()

    def wait_scatter(s):
        pltpu.make_async_copy(ybuf.at[s], ybuf.at[s], ssem.at[s]).wait()

    @pl.when((j >= 2) & (j - 2 < nvalid))
    def _():
        wait_scatter(slot)

    @pl.when(j < nvalid)
    def _():
        wait_gather(slot)
        xb = xbuf[slot]
        h = _silu(_dot(xb, w1b[...])) * _dot(xb, w3b[...])
        ybuf[slot] = _dot(h, w2b[...])
        for_rows(lambda r: row_copy_out(j, slot, r).start())

    @pl.when(j >= nvalid)
    def _():
        ybuf[slot] = jnp.zeros((bm, ybuf.shape[2]), F32)
        fill = pltpu.make_async_copy(ybuf.at[slot], y_hbm.at[pl.ds(j * bm, bm)], ssem.at[slot])
        fill.start()
        fill.wait()

    @pl.when(j == nb - 1)
    def _():
        @pl.when((j >= 1) & (j - 1 < nvalid))
        def _():
            wait_scatter(1 - slot)

        @pl.when(j < nvalid)
        def _():
            wait_scatter(slot)


def _moe_dispatch(slab, t, bm):
    n = 2 * t
    e = N_EXPERTS
    rows = -(-(n + e * (bm - 1)) // bm) * bm
    ids = slab[:, :2].astype(jnp.int32).reshape(n)
    order = jnp.argsort(ids, stable=True).astype(jnp.int32)
    counts = jnp.sum((ids[:, None] == jnp.arange(e, dtype=jnp.int32)[None, :]).astype(jnp.int32), 0)
    padded = -(-counts // bm) * bm
    pend = jnp.cumsum(padded)
    pstart = pend - padded
    start = jnp.cumsum(counts) - counts
    n_blocks = rows // bm
    bstart = jnp.arange(n_blocks, dtype=jnp.int32) * bm
    block_exp = jnp.minimum(jnp.sum((bstart[:, None] >= pend[None, :]).astype(jnp.int32), 1), e - 1)
    nvalid = (pend[-1] // bm).astype(jnp.int32).reshape(1)
    per_row = lambda tbl: jnp.repeat(tbl[block_exp], bm)
    local = jnp.arange(rows, dtype=jnp.int32) - per_row(pstart)
    valid = local < per_row(counts)
    a = order[jnp.clip(per_row(start) + local, 0, n - 1)]
    pad_rank = jnp.cumsum(1 - valid.astype(jnp.int32)) - 1
    row_src = jnp.where(valid, a // 2, 0)
    row_dst = jnp.where(valid, (a % 2) * t + a // 2, n + pad_rank)
    packed = row_src | (row_dst << TOK_BITS)
    return packed.astype(jnp.int32), block_exp, nvalid, rows


def _moe_ffn(x, slab, w1, w3, w2, layer, bm=MOE_BM):
    t, d = x.shape
    de = w1.shape[3]
    assert t <= (1 << TOK_BITS)
    packed, block_exp, nvalid, rows = _moe_dispatch(slab, t, bm)
    assert rows < (1 << (32 - TOK_BITS))
    n_blocks = rows // bm
    wmap = lambda j, bexp, nv, pk: (layer, bexp[j], 0, 0)
    grid_spec = pltpu.PrefetchScalarGridSpec(
        num_scalar_prefetch=3,
        grid=(n_blocks,),
        in_specs=[pl.BlockSpec(memory_space=pl.ANY),
                  pl.BlockSpec((None, None, d, de), wmap), pl.BlockSpec((None, None, d, de), wmap),
                  pl.BlockSpec((None, None, de, d), wmap)],
        out_specs=pl.BlockSpec(memory_space=pl.ANY),
        scratch_shapes=[pltpu.VMEM((2, bm, d), F32), pltpu.VMEM((2, bm, d), F32),
                        pltpu.VMEM((d, de), BF16), pltpu.VMEM((d, de), BF16), pltpu.VMEM((de, d), BF16),
                        pltpu.SemaphoreType.DMA((2,)), pltpu.SemaphoreType.DMA((2,))],
    )
    return pl.pallas_call(
        functools.partial(_ffn_kernel, bm=bm),
        name="moe_ffn",
        grid_spec=grid_spec,
        out_shape=jax.ShapeDtypeStruct((rows, d), F32),
        compiler_params=_params("arbitrary"),
    )(block_exp, nvalid, packed, x, w1, w3, w2)


def _moe_ln_kernel(x_ref, y0_ref, y1_ref, s_ref, g_ref, b_ref, o_ref, oh_ref, *, alpha):
    s = s_ref[...]
    y = alpha * x_ref[...] + s[:, 2:3] * y0_ref[...] + s[:, 3:4] * y1_ref[...]
    o = _layer_norm(y, g_ref[...], b_ref[...])
    o_ref[...] = o
    oh_ref[...] = o.astype(BF16)


def _moe_ln(x, y, slab, g, b, alpha, tm=512):
    t, d = x.shape
    tm = min(tm, t)
    nt = t // tm
    row = lambda i: (i, 0)
    fix = lambda i: (0, 0)
    return pl.pallas_call(
        functools.partial(_moe_ln_kernel, alpha=alpha),
        name="moe_ln",
        grid=(nt,),
        in_specs=[pl.BlockSpec((tm, d), row), pl.BlockSpec((tm, d), row),
                  pl.BlockSpec((tm, d), lambda i: (i + nt, 0)), pl.BlockSpec((tm, LANES), row),
                  pl.BlockSpec((1, d), fix), pl.BlockSpec((1, d), fix)],
        out_specs=[pl.BlockSpec((tm, d), row), pl.BlockSpec((tm, d), row)],
        out_shape=[jax.ShapeDtypeStruct((t, d), F32), jax.ShapeDtypeStruct((t, d), BF16)],
        compiler_params=_params("parallel"),
    )(x, y, y, slab, g.reshape(1, d), b.reshape(1, d))


def _moe_sublayer(x, slab, w1, w3, w2, layer, g, b, alpha):
    y = _moe_ffn(x, slab, w1, w3, w2, layer)
    return _moe_ln(x, y, slab, g, b, alpha)


def _attn_kernel(q_ref, kp_ref, kc_ref, vp_ref, vc_ref, o_ref, k_all, v_all, q_all, acc_o, acc_m, acc_l, *, rows):
    n = pl.program_id(2)
    k_all[0:rows, :] = kp_ref[...]
    k_all[rows:2 * rows, :] = kc_ref[...]
    v_all[0:rows, :] = vp_ref[...]
    v_all[rows:2 * rows, :] = vc_ref[...]
    q_all[...] = q_ref[...] * (HEAD_DIM ** -0.5)
    nk = ATT_BLOCK
    qi = lax.broadcasted_iota(jnp.int32, (nk, 2 * nk), 0)
    kj = lax.broadcasted_iota(jnp.int32, (nk, 2 * nk), 1)
    band = (kj >= qi) & (kj <= qi + nk)
    lane = lax.broadcasted_iota(jnp.int32, (nk, LANES), 1)
    head1 = lane >= HEAD_DIM

    for p_idx, (window, dil) in enumerate(C_PAIRS):
        assert window // dil == nk
        span = nk * dil
        n_sub = rows // span

        def group(gi, carry, dil=dil, span=span, n_sub=n_sub, first=(p_idx == 0)):
            sel = lambda ref, start: ref[pl.ds(start, nk, stride=dil), :]
            loaded = []
            for i in range(ATT_GROUP):
                u = gi * ATT_GROUP + i
                res = u // n_sub
                sub = u - res * n_sub
                qstart = sub * span + res
                q = sel(q_all, qstart)
                k2 = jnp.concatenate([sel(k_all, rows + qstart - span), sel(k_all, rows + qstart)], 0).astype(BF16)
                v2 = jnp.concatenate([sel(v_all, rows + qstart - span), sel(v_all, rows + qstart)], 0).astype(BF16)
                dst = pl.ds(qstart, nk, stride=dil)
                old = None if first else (acc_o[dst, :], acc_m[dst, :], acc_l[dst, :])
                loaded.append((q, k2, v2, dst, old, sub))
            results = []
            for q, k2, v2, dst, old, sub in loaded:
                ok = band & ((kj >= nk) | (n > 0) | (sub > 0))
                outs = []
                for h1 in (False, True):
                    hm = head1 if h1 else jnp.logical_not(head1)
                    sc = _dot(jnp.where(hm, q, 0.0), k2, NT)
                    sc = jnp.where(ok, sc, -jnp.inf)
                    m = jnp.max(sc, -1, keepdims=True)
                    p = jnp.exp(sc - m)
                    outs.append((_dot(p, v2), m, jnp.sum(p, -1, keepdims=True)))
                o = jnp.where(head1, outs[1][0], outs[0][0])
                m = jnp.where(head1, outs[1][1], outs[0][1])
                l = jnp.where(head1, outs[1][2], outs[0][2])
                if old is not None:
                    o_old, m_old, l_old = old
                    m_new = jnp.maximum(m_old, m)
                    a_old = jnp.exp(m_old - m_new)
                    a_cur = jnp.exp(m - m_new)
                    o = o_old * a_old + o * a_cur
                    l = l_old * a_old + l * a_cur
                    m = m_new
                results.append((dst, o, m, l))
            for dst, o, m, l in results:
                acc_o[dst, :] = o
                acc_m[dst, :] = m
                acc_l[dst, :] = l
            return carry

        assert (dil * n_sub) % ATT_GROUP == 0
        lax.fori_loop(0, dil * n_sub // ATT_GROUP, group, 0)

    o_ref[...] = acc_o[...] / acc_l[...]


def _dilated_attention(p, b, s, qoff, koff, voff):
    rows = max(w for w, _ in C_PAIRS)
    assert s % rows == 0
    nblk = s // rows
    cur = lambda off: pl.BlockSpec((rows, LANES), lambda bi, hp, n: (bi * nblk + n, off + hp))
    prev = lambda off: pl.BlockSpec((rows, LANES), lambda bi, hp, n: (bi * nblk + jnp.maximum(n - 1, 0), off + hp))
    scratch = [pltpu.VMEM((2 * rows, LANES), F32), pltpu.VMEM((2 * rows, LANES), F32)] + \
              [pltpu.VMEM((rows, LANES), F32)] * 4
    return pl.pallas_call(
        functools.partial(_attn_kernel, rows=rows),
        name="dil_attn",
        grid=(b, C_WIDTH // LANES, nblk),
        in_specs=[cur(qoff), prev(koff), cur(koff), prev(voff), cur(voff)],
        out_specs=pl.BlockSpec((rows, LANES), lambda bi, hp, n: (bi * nblk + n, hp)),
        out_shape=jax.ShapeDtypeStruct((b * s, C_WIDTH), F32),
        scratch_shapes=scratch,
        compiler_params=_params("parallel", "parallel", "arbitrary"),
    )(p, p, p, p, p)


def _causal_conv(x_ref, pad_ref, w_ref, b_ref, first, blk):
    @pl.when(first)
    def _():
        pad_ref[0:8, :] = jnp.zeros((8, pad_ref.shape[1]), F32)
    pad_ref[8:8 + blk, :] = x_ref[...]
    w = w_ref[...]
    acc = b_ref[...]
    for j in range(D_CONV):
        acc = acc + w[j:j + 1, :] * pad_ref[pl.ds(8 - (D_CONV - 1) + j, blk), :]
    pad_ref[0:8, :] = pad_ref[blk:blk + 8, :]
    return acc


def _mlstm_kernel(q_ref, k_ref, v_ref, og_ref, gate_ref, cwq_ref, cwk_ref, cbq_ref, cbk_ref, gb_ref,
                  o_ref, qpad, kpad, c_state, n_state, m_state, *, blk, chunk):
    first = pl.program_id(1) == 0

    @pl.when(first)
    def _():
        c_state[...] = jnp.zeros_like(c_state)
        n_state[...] = jnp.zeros_like(n_state)
        m_state[...] = jnp.zeros_like(m_state)

    qc = _silu(_causal_conv(q_ref, qpad, cwq_ref, cbq_ref, first, blk))
    kc = _silu(_causal_conv(k_ref, kpad, cwk_ref, cbk_ref, first, blk)) * (D_DQK ** -0.5)
    gates = gate_ref[...] + gb_ref[...]
    logf = _log_sigmoid(gates)
    lane = lax.broadcasted_iota(jnp.int32, (chunk, LANES), 1)
    ti = lax.broadcasted_iota(jnp.int32, (chunk, chunk), 0)
    si = lax.broadcasted_iota(jnp.int32, (chunk, chunk), 1)
    causal = si <= ti
    tri = causal.astype(F32)
    pick = (lax.broadcasted_iota(jnp.int32, (2 * D_HEADS, LANES), 0)
            == lax.broadcasted_iota(jnp.int32, (2 * D_HEADS, LANES), 1)).astype(F32)

    c_mat = [c_state[h] for h in range(D_HEADS)]
    n_vec = [n_state[h:h + 1, :] for h in range(D_HEADS)]
    m_prev = [m_state[h:h + 1, 0:1] for h in range(D_HEADS)]
    for c in range(blk // chunk):
        rows = slice(c * chunk, (c + 1) * chunk)
        gi = gates[rows]
        fcum_all = _dot_exact_lhs(tri, logf[rows])
        f_rows = _dot_exact_lhs(pick, fcum_all, NT)
        i_rows = _dot_exact_lhs(pick, gi, NT)
        for h in range(D_HEADS):
            sel_i = lane == h
            sel_f = lane == D_HEADS + h
            qs = slice(h * D_DQK, (h + 1) * D_DQK)
            vs = slice(h * D_DV, (h + 1) * D_DV)
            q_c, k_c, v_c = qc[rows, qs], kc[rows, qs], v_ref[rows, vs]
            f_col = jnp.sum(jnp.where(sel_f, fcum_all, 0.0), -1, keepdims=True)
            i_col = jnp.sum(jnp.where(sel_i, gi, 0.0), -1, keepdims=True)
            f_row = f_rows[D_HEADS + h:D_HEADS + h + 1, :]
            i_row = i_rows[h:h + 1, :]
            log_d = jnp.where(causal, f_col - f_row + i_row, -jnp.inf)
            log_inter = f_col + m_prev[h]
            m_t = jnp.maximum(jnp.max(log_d, -1, keepdims=True), log_inter)
            w_inter = jnp.exp(log_inter - m_t)
            scores = _dot(q_c, k_c, NT) * jnp.exp(log_d - m_t)
            num = w_inter * _dot(q_c, c_mat[h]) + _dot(scores, v_c)
            den = w_inter * jnp.sum(q_c * n_vec[h], -1, keepdims=True) + jnp.sum(scores, -1, keepdims=True)
            out = num / jnp.maximum(jnp.abs(den), jnp.exp(-m_t))
            o_ref[rows, vs] = _sigmoid(og_ref[rows, vs]) * out
            f_tot = f_col[chunk - 1:chunk]
            log_w = f_tot - f_col + i_col
            m_new = jnp.maximum(f_tot + m_prev[h], jnp.max(log_w, 0, keepdims=True))
            kw = k_c * jnp.exp(log_w - m_new)
            w_old = jnp.exp(f_tot + m_prev[h] - m_new)
            c_mat[h] = w_old * c_mat[h] + _dot(kw.T, v_c)
            n_vec[h] = w_old * n_vec[h] + jnp.sum(kw, 0, keepdims=True)
            m_prev[h] = m_new
    for h in range(D_HEADS):
        c_state[h] = c_mat[h]
        n_state[h:h + 1, :] = n_vec[h]
        m_state[h:h + 1, :] = jnp.broadcast_to(m_prev[h], (1, LANES))


def _mlstm(p, b, s, off, conv_w, conv_b, b_i, b_f, blk=SEQ_BLOCK, chunk=ML_CHUNK):
    blk = min(blk, s)
    nblk = s // blk
    hq = D_QK_WIDTH // LANES
    hv = D_WIDTH // LANES
    assert off % hq == 0 and (off + 2 * hq) % hv == 0
    rowblk = lambda bi, n: bi * nblk + n
    qk_spec = lambda o: pl.BlockSpec((blk, D_QK_WIDTH), lambda bi, n: (rowblk(bi, n), off // hq + o))
    v_spec = lambda o: pl.BlockSpec((blk, D_WIDTH), lambda bi, n: (rowblk(bi, n), (off + 2 * hq) // hv + o))
    par_spec = lambda rows, o: pl.BlockSpec((rows, D_QK_WIDTH), lambda bi, n: (0, o))
    gate_off = off + 2 * hq + 2 * hv
    gbias = jnp.pad(jnp.concatenate([b_i, b_f]), (0, LANES - 2 * D_HEADS)).reshape(1, LANES)
    cb = conv_b.reshape(1, -1)
    return pl.pallas_call(
        functools.partial(_mlstm_kernel, blk=blk, chunk=chunk),
        name="mlstm",
        grid=(b, nblk),
        in_specs=[qk_spec(0), qk_spec(1), v_spec(0), v_spec(1),
                  pl.BlockSpec((blk, LANES), lambda bi, n: (rowblk(bi, n), gate_off)),
                  par_spec(D_CONV, 0), par_spec(D_CONV, 1), par_spec(1, 0), par_spec(1, 1),
                  pl.BlockSpec((1, LANES), lambda bi, n: (0, 0))],
        out_specs=pl.BlockSpec((blk, D_WIDTH), lambda bi, n: (rowblk(bi, n), 0)),
        out_shape=jax.ShapeDtypeStruct((b * s, D_WIDTH), F32),
        scratch_shapes=[pltpu.VMEM((blk + 8, D_QK_WIDTH), F32), pltpu.VMEM((blk + 8, D_QK_WIDTH), F32),
                        pltpu.VMEM((D_HEADS, D_DQK, D_DV), F32), pltpu.VMEM((8, LANES), F32),
                        pltpu.VMEM((8, LANES), F32)],
        compiler_params=_params("parallel", "arbitrary"),
    )(p, p, p, p, p, conv_w, conv_w, cb, cb, gbias)


TN = (((0,), (0,)), ((), ()))


def _chunk_masks(blk, chunk):
    t = lax.broadcasted_iota(jnp.int32, (blk, blk), 0)
    s = lax.broadcasted_iota(jnp.int32, (blk, blk), 1)
    same = (t // chunk) == (s // chunk)
    return same, same & (s <= t), same & (s < t)


def _hgrn2_kernel(q_ref, f_ref, i_ref, g_ref, lb_ref, nw_ref, o_ref, s_state, o_scr, *, blk, chunk, wide):
    @pl.when(pl.program_id(2) == 0)
    def _():
        s_state[...] = jnp.zeros_like(s_state)

    lb = lb_ref[...]
    f = f_ref[...]
    q = _silu(q_ref[...])
    v = i_ref[...]
    e = jnp.exp(-jnp.abs(f))
    a = jnp.log(lb)
    bb = jnp.log1p(-lb) + jnp.minimum(f, 0.0) - jnp.log1p(e)
    log_f = jnp.maximum(a, bb) + jnp.log1p(jnp.exp(-jnp.abs(a - bb)))
    k = (1.0 - lb) * jnp.where(f >= 0.0, e, 1.0) / (1.0 + e)

    _, tri_w, _ = _chunk_masks(blk, wide)
    bw = _dot_exact_lhs(tri_w.astype(F32), log_f)
    wchunks = [bw[c * wide:(c + 1) * wide] for c in range(blk // wide)]
    mids = [bc[wide // 2 - 1:wide // 2] for bc in wchunks]
    spread = [jnp.max(jnp.abs(bc - m)) for bc, m in zip(wchunks, mids)]
    safe = functools.reduce(jnp.maximum, spread) < GLA_SAFE_LOG

    ti = lax.broadcasted_iota(jnp.int32, (wide, wide), 0)
    si = lax.broadcasted_iota(jnp.int32, (wide, wide), 1)
    st = s_state[...]
    for c, (bc, m) in enumerate(zip(wchunks, mids)):
        rows = slice(c * wide, (c + 1) * wide)
        q_c, k_c, v_c = q[rows], k[rows], v[rows]
        total = bc[wide - 1:wide]
        att = jnp.where(si <= ti, _dot(q_c * jnp.exp(bc - m), k_c * jnp.exp(m - bc), NT), 0.0)
        o_scr[rows, :] = _dot(att, v_c) + _dot(q_c * jnp.exp(bc), st, NT)
        st = st * jnp.exp(total) + _dot(v_c, k_c * jnp.exp(total - bc), TN)

    @pl.when(safe)
    def _():
        s_state[...] = st

    @pl.when(jnp.logical_not(safe))
    def _():
        same, tri_incl, _ = _chunk_masks(blk, chunk)
        bcum = _dot_exact_lhs(tri_incl.astype(F32), log_f)
        blast = _dot_exact_lhs(same.astype(F32), log_f)
        qe = q * jnp.exp(bcum)
        kd = k * jnp.exp(blast - bcum)
        decay = jnp.exp(blast)
        pos = lax.broadcasted_iota(jnp.int32, (chunk, LANES), 0)
        st = s_state[...]
        for c in range(blk // chunk):
            rows = slice(c * chunk, (c + 1) * chunk)
            q_c, k_c, v_c, b_c = q[rows], k[rows], v[rows], bcum[rows]
            o_c = _dot(qe[rows], st, NT) + jnp.sum(q_c * k_c, -1, keepdims=True) * v_c
            for dist in range(1, chunk):
                e = jnp.exp(jnp.where(pos >= dist, b_c - pltpu.roll(b_c, dist, 0), -jnp.inf))
                w = jnp.sum(q_c * pltpu.roll(k_c, dist, 0) * e, -1, keepdims=True)
                o_c = o_c + w * pltpu.roll(v_c, dist, 0)
            o_scr[rows, :] = o_c
            st = st * decay[c * chunk:c * chunk + 1] + _dot(v_c, kd[rows], TN)
        s_state[...] = st

    o = o_scr[...]
    o = o * lax.rsqrt(jnp.mean(o * o, -1, keepdims=True) + RMS_EPS) * nw_ref[...]
    o_ref[...] = o * _silu(g_ref[...])


def _hgrn2(p, b, s, off, lb, norm_w, blk=SEQ_BLOCK, chunk=GLA_CHUNK):
    blk = min(blk, s)
    nblk = s // blk
    spec = lambda o: pl.BlockSpec((blk, LANES), lambda bi, h, n: (bi * nblk + n, off + o * B_HEADS + h))
    return pl.pallas_call(
        functools.partial(_hgrn2_kernel, blk=blk, chunk=chunk, wide=min(GLA_WIDE_CHUNK, blk)),
        name="hgrn2",
        grid=(b, B_HEADS, nblk),
        in_specs=[spec(0), spec(1), spec(2), spec(3),
                  pl.BlockSpec((1, LANES), lambda bi, h, n: (0, h)),
                  pl.BlockSpec((1, LANES), lambda bi, h, n: (0, 0))],
        out_specs=pl.BlockSpec((blk, LANES), lambda bi, h, n: (bi * nblk + n, h)),
        out_shape=jax.ShapeDtypeStruct((b * s, B_WIDTH), F32),
        scratch_shapes=[pltpu.VMEM((B_DV, B_DK), F32), pltpu.VMEM((blk, LANES), F32)],
        compiler_params=_params("parallel", "parallel", "arbitrary"),
    )(p, p, p, p, lb.reshape(1, -1), norm_w.reshape(1, -1))


def _rwkv7_kernel(r_ref, k_ref, v_ref, wa_ref, xg_ref, mur_ref, muk_ref, muv_ref, muwa_ref, mug_ref,
                  w0_ref, a0_ref, kk_ref, ka_ref, rk_ref, lnw_ref, lnb_ref, w2_ref, a2_ref, g2_ref,
                  o_ref, tail, s_state, *, blk, chunk, npp):
    @pl.when(pl.program_id(2) == 0)
    def _():
        tail[...] = jnp.zeros_like(tail)
        s_state[...] = jnp.zeros_like(s_state)

    row0 = lax.broadcasted_iota(jnp.int32, (blk, LANES), 0) == 0

    def shift_lerp(idx, x, mu):
        prev = tail[0:1, idx * LANES:(idx + 1) * LANES]
        shifted = jnp.where(row0, prev, pltpu.roll(x, 1, 0))
        tail[0:1, idx * LANES:(idx + 1) * LANES] = x[blk - 1:blk]
        return x + (shifted - x) * mu

    wa = shift_lerp(3 * npp, wa_ref[...], muwa_ref[...])
    xg = shift_lerp(3 * npp + 1, xg_ref[...], mug_ref[...])
    tanh_wa = jnp.tanh(wa)
    sig_xg = _sigmoid(xg)

    lane = lax.broadcasted_iota(jnp.int32, (LANES, LANES), 1)
    sub = lax.broadcasted_iota(jnp.int32, (LANES, LANES), 0)
    head_bd = (lane // HEAD_DIM) == (sub // HEAD_DIM)
    first_head = lax.broadcasted_iota(jnp.int32, (blk, LANES), 1) < HEAD_DIM

    def head_sum(x):
        s0 = jnp.sum(jnp.where(first_head, x, 0.0), -1, keepdims=True)
        s1 = jnp.sum(jnp.where(first_head, 0.0, x), -1, keepdims=True)
        return jnp.where(first_head, s0, s1)

    _, tri_incl, _ = _chunk_masks(blk, chunk)
    tri_incl = tri_incl.astype(F32)
    c2 = 2 * chunk
    lane_c = lax.broadcasted_iota(jnp.int32, (chunk, LANES), 1)
    h1 = lane_c >= HEAD_DIM
    h0 = jnp.logical_not(h1)
    stack = lambda x: jnp.concatenate([jnp.where(h0, x, 0.0), jnp.where(h1, x, 0.0)], 0)
    ri = lax.broadcasted_iota(jnp.int32, (2 * c2, 2 * c2), 0)
    ci = lax.broadcasted_iota(jnp.int32, (2 * c2, 2 * c2), 1)
    amask = ((ci % chunk) < (ri % chunk)) | (((ci % chunk) == (ri % chunk)) & (ri >= c2))
    eye = (lax.broadcasted_iota(jnp.int32, (c2, c2), 0) == lax.broadcasted_iota(jnp.int32, (c2, c2), 1)).astype(F32)
    nchunk = blk // chunk

    pairs = []
    for pi in range(npp):
        ls = slice(pi * LANES, (pi + 1) * LANES)
        r = shift_lerp(3 * pi, r_ref[:, ls], mur_ref[:, ls])
        k = shift_lerp(3 * pi + 1, k_ref[:, ls], muk_ref[:, ls])
        v = shift_lerp(3 * pi + 2, v_ref[:, ls], muv_ref[:, ls])
        pre_w = -(w0_ref[:, ls] + _dot(tanh_wa, w2_ref[:, ls]))
        w_log = -(jnp.maximum(pre_w, 0.0) + jnp.log1p(jnp.exp(-jnp.abs(pre_w)))) - 0.5
        lw = -jnp.exp(w_log)
        a = _sigmoid(a0_ref[:, ls] + _dot(wa, a2_ref[:, ls]))
        g = _dot(sig_xg, g2_ref[:, ls])
        kk = k * kk_ref[:, ls]
        kk = kk / jnp.maximum(jnp.sqrt(head_sum(kk * kk)), 1e-12)
        k = k * (1.0 + (a - 1.0) * ka_ref[:, ls])
        bonus = head_sum(r * k * rk_ref[:, ls]) * v
        av = -kk
        bv = kk * a
        lcum = _dot_exact_lhs(tri_incl, lw)
        lsum = jnp.concatenate([jnp.broadcast_to(lcum[(c + 1) * chunk - 1:(c + 1) * chunk], (chunk, LANES))
                                for c in range(nchunk)], 0)
        a_t = av * jnp.exp(lcum - lw)
        r_t = r * jnp.exp(lcum)
        inv = jnp.exp(-lcum)
        b_t = bv * inv
        k_t = k * inv
        to_end = jnp.exp(lsum - lcum)
        ars, amats = [], []
        for c in range(nchunk):
            rows = slice(c * chunk, (c + 1) * chunk)
            ar = jnp.concatenate([stack(a_t[rows]), stack(r_t[rows])], 0)
            bk = jnp.concatenate([stack(b_t[rows]), stack(k_t[rows])], 0)
            ars.append(ar)
            amats.append(jnp.where(amask, _dot(ar, bk, NT), 0.0))
        pairs.append(dict(v=v, g=g, bonus=bonus, b_c=bv * to_end, k_c=k * to_end, gam=jnp.exp(lsum),
                          ars=ars, amats=amats))

    keys = [(pi, c) for pi in range(npp) for c in range(nchunk)]
    xs = {key: pairs[key[0]]["amats"][key[1]][:c2, :c2] for key in keys}
    t_inv = {key: eye + xs[key] for key in keys}
    for _ in range(chunk.bit_length() - 2):
        for key in keys:
            xs[key] = _dot(xs[key], xs[key])
        for key in keys:
            t_inv[key] = t_inv[key] + _dot(t_inv[key], xs[key])

    sts = [s_state[pi] for pi in range(npp)]
    ys = [[] for _ in range(npp)]
    for c in range(nchunk):
        rows = slice(c * chunk, (c + 1) * chunk)
        for pi in range(npp):
            pr = pairs[pi]
            ar, amat = pr["ars"][c], pr["amats"][c]
            v_c = pr["v"][rows]
            v2 = jnp.concatenate([v_c, v_c], 0)
            g1 = _dot(ar, sts[pi], NT)
            u2 = _dot(t_inv[(pi, c)], g1[:c2] + _dot(amat[:c2, c2:], v2))
            y2 = g1[c2:] + _dot(amat[c2:], jnp.concatenate([u2, v2], 0))
            ys[pi].append(jnp.where(h1, y2[chunk:], y2[:chunk]))
            u = jnp.where(h1, u2[chunk:], u2[:chunk])
            upd = _dot(jnp.concatenate([u, v_c], 0), jnp.concatenate([pr["b_c"][rows], pr["k_c"][rows]], 0), TN)
            sts[pi] = jnp.where(head_bd, sts[pi] * pr["gam"][c * chunk:c * chunk + 1] + upd, 0.0)

    for pi in range(npp):
        ls = slice(pi * LANES, (pi + 1) * LANES)
        s_state[pi] = sts[pi]
        y = jnp.concatenate(ys[pi], 0)
        mu = head_sum(y) * (1.0 / HEAD_DIM)
        yc = y - mu
        var = head_sum(yc * yc) * (1.0 / HEAD_DIM)
        y = yc * lax.rsqrt(var + A_GN_EPS) * lnw_ref[:, ls] + lnb_ref[:, ls] + pairs[pi]["bonus"]
        o_ref[:, ls] = y * pairs[pi]["g"]


def _rwkv7(p, b, s, mu, w0, w2, a0, a2, g2, k_k, k_a, r_k, lnx_w, lnx_b, blk=SEQ_BLOCK, chunk=RW_CHUNK):
    blk = min(blk, s)
    nblk = s // blk
    npp = RW_PAIRS
    wide = npp * LANES
    ngrp = A_WIDTH // wide
    lora = 3 * A_WIDTH // LANES
    act = lambda o: pl.BlockSpec((blk, wide), lambda bi, hp, n: (bi * nblk + n, o * ngrp + hp))
    act_fix = lambda o: pl.BlockSpec((blk, LANES), lambda bi, hp, n: (bi * nblk + n, o))
    vec = lambda o: pl.BlockSpec((1, wide), lambda bi, hp, n: (0, o * ngrp + hp))
    vec_fix = lambda o: pl.BlockSpec((1, LANES), lambda bi, hp, n: (0, o))
    mat = pl.BlockSpec((LANES, wide), lambda bi, hp, n: (0, hp))
    row = lambda t: t.reshape(1, -1)
    w2p = jnp.concatenate([w2, jnp.zeros_like(a2)], 0)
    a2p = jnp.concatenate([jnp.zeros_like(w2), a2], 0)
    mu2 = row(mu)
    return pl.pallas_call(
        functools.partial(_rwkv7_kernel, blk=blk, chunk=chunk, npp=npp),
        name="rwkv7",
        grid=(b, ngrp, nblk),
        in_specs=[act(0), act(1), act(2), act_fix(lora), act_fix(lora + 1),
                  vec(0), vec(1), vec(2), vec_fix(lora), vec_fix(lora + 1),
                  vec(0), vec(0), vec(0), vec(0), vec(0), vec(0), vec(0), mat, mat, mat],
        out_specs=pl.BlockSpec((blk, wide), lambda bi, hp, n: (bi * nblk + n, hp)),
        out_shape=jax.ShapeDtypeStruct((b * s, A_WIDTH), F32),
        scratch_shapes=[pltpu.VMEM((8, (3 * npp + 2) * LANES), F32), pltpu.VMEM((npp, LANES, LANES), F32)],
        compiler_params=_params("parallel", "parallel", "arbitrary"),
    )(p, p, p, p, p, mu2, mu2, mu2, mu2, mu2,
      row(w0), row(a0), row(k_k), row(k_a), row(r_k), row(lnx_w), row(lnx_b), w2p, a2p, g2)


def kernel(x, ev_w_in, ev_w_out, rwkv_mu, rwkv_w0, rwkv_w2, rwkv_a0, rwkv_a2, rwkv_g2, rwkv_kk, rwkv_ka, rwkv_rk,
           rwkv_lnx_w, rwkv_lnx_b, hgrn_lb, hgrn_norm_w, od_w_in, od_w_out, mlstm_conv_w, mlstm_conv_b,
           mlstm_b_i, mlstm_b_f, ln_w, ln_b, moe_wg, moe_bg, moe_we, moe_be, moe_w1, moe_w3, moe_w2):
    b, s, d = x.shape
    depth = ln_w.shape[0]
    alpha = (2 * depth) ** 0.25
    lb_all = jnp.cumsum(jax.nn.softmax(hgrn_lb.astype(F32), axis=0), axis=0)
    lb_all = lb_all - lb_all[:1]
    tn = 2 * LANES
    xf = x.reshape(b * s, d)
    xh = xf
    for layer in range(depth):
        j = layer // 2
        if layer % 2 == 0:
            p = _matmul(xh, ev_w_in[j].astype(BF16), tn=tn)
            ya = _rwkv7(p, b, s, rwkv_mu[j], rwkv_w0[j], rwkv_w2[j], rwkv_a0[j], rwkv_a2[j], rwkv_g2[j],
                        rwkv_kk[j], rwkv_ka[j], rwkv_rk[j], rwkv_lnx_w[j], rwkv_lnx_b[j])
            yb = _hgrn2(p, b, s, A_COLS // LANES, lb_all[j], hgrn_norm_w[j])
            w_out = ev_w_out[j]
        else:
            w_in = od_w_in[j]
            w_in = jnp.pad(w_in, ((0, 0), (0, -w_in.shape[1] % tn))).astype(BF16)
            p = _matmul(xh, w_in, tn=tn)
            nq = C_WIDTH // LANES
            ya = _dilated_attention(p, b, s, 0, nq, 2 * nq)
            yb = _mlstm(p, b, s, 3 * nq, mlstm_conv_w[j], mlstm_conv_b[j], mlstm_b_i[j], mlstm_b_f[j])
            w_out = od_w_out[j]
        router = _router_weights(moe_wg[layer], moe_bg[layer], moe_we[layer], moe_be[layer])
        xf, slab = _proj_ln(ya, yb, w_out, xf, ln_w[layer, 0], ln_b[layer, 0], router, alpha)
        xf, xh = _moe_sublayer(xf, slab, moe_w1, moe_w3, moe_w2, layer, ln_w[layer, 1], ln_b[layer, 1], alpha)
    return xf.reshape(b, s, d)
```

```python
import functools

import jax
import jax.numpy as jnp
from jax import lax
from jax.experimental import pallas as pl
from jax.experimental.pallas import tpu as pltpu

F32 = jnp.float32
BF16 = jnp.bfloat16

HEAD_DIM = 64
A_HEADS = 16
A_WIDTH = A_HEADS * HEAD_DIM
A_LORA_W = 64
A_LORA_A = 64
A_LORA_G = 128
A_COLS = 3 * A_WIDTH + A_LORA_W + A_LORA_A + A_LORA_G
A_GN_EPS = 64e-5
B_HEADS = 8
B_DK = 128
B_DV = 128
B_WIDTH = B_HEADS * B_DV
C_HEADS = 16
C_WIDTH = C_HEADS * HEAD_DIM
C_PAIRS = ((128, 1), (512, 4), (2048, 16))
D_HEADS = 4
D_DQK = 128
D_DV = 256
D_QK_WIDTH = D_HEADS * D_DQK
D_WIDTH = D_HEADS * D_DV
D_CONV = 4
ODD_COLS = 3 * C_WIDTH + 2 * D_QK_WIDTH + 2 * D_WIDTH + 2 * D_HEADS
N_GROUPS = 4
EXPERTS_PER_GROUP = 8
N_EXPERTS = N_GROUPS * EXPERTS_PER_GROUP
LN_EPS = 1e-5
RMS_EPS = 1e-6

LANES = 128
VMEM_LIMIT = 56 * 1024 * 1024
RW_CHUNK = 64
RW_PAIRS = 4
GLA_WIDE_CHUNK = 64
GLA_SAFE_LOG = 60.0
GLA_CHUNK = 16
ML_CHUNK = 256
SEQ_BLOCK = 256
ATT_BLOCK = 128
ATT_GROUP = 16
IN_PROJ_TILE_BYTES = 16 * 1024 * 1024
PROJ_SUB = 128
MOE_BM = 256
TOK_BITS = 15

NN = (((1,), (0,)), ((), ()))
NT = (((1,), (1,)), ((), ()))


def _dot(a, b, dims=NN):
    return lax.dot_general(a.astype(BF16), b.astype(BF16), dims, preferred_element_type=F32)


def _split3(x):
    hi = x.astype(BF16)
    r1 = x - hi.astype(F32)
    mid = r1.astype(BF16)
    lo = (r1 - mid.astype(F32)).astype(BF16)
    return hi, mid, lo


def _dot_exact_lhs(mask, x, dims=NN):
    hi, mid, lo = _split3(x)
    m = mask.astype(BF16)
    d = lambda p: lax.dot_general(m, p, dims, preferred_element_type=F32)
    return d(hi) + d(mid) + d(lo)


def _dot_exact_rhs(x, mask):
    hi, mid, lo = _split3(x)
    m = mask.astype(BF16)
    d = lambda p: lax.dot_general(p, m, NN, preferred_element_type=F32)
    return d(hi) + d(mid) + d(lo)


def _sigmoid(x):
    return 1.0 / (1.0 + jnp.exp(-x))


def _silu(x):
    return x * _sigmoid(x)


def _log_sigmoid(x):
    return jnp.minimum(x, 0.0) - jnp.log1p(jnp.exp(-jnp.abs(x)))


def _params(*sem):
    return pltpu.CompilerParams(dimension_semantics=sem, vmem_limit_bytes=VMEM_LIMIT)


def _mm_kernel(x_ref, w_ref, o_ref):
    o_ref[...] = _dot(x_ref[...], w_ref[...])


def _matmul(x, w, tn=256):
    t, k = x.shape
    n = w.shape[1]
    tm = min(IN_PROJ_TILE_BYTES // (k * x.dtype.itemsize), t)
    return pl.pallas_call(
        _mm_kernel,
        name="in_proj",
        grid=(t // tm, n // tn),
        in_specs=[pl.BlockSpec((tm, k), lambda i, j: (i, 0)),
                  pl.BlockSpec((k, tn), lambda i, j: (0, j))],
        out_specs=pl.BlockSpec((tm, tn), lambda i, j: (i, j)),
        out_shape=jax.ShapeDtypeStruct((t, n), F32),
        compiler_params=_params("parallel", "arbitrary"),
    )(x, w)


def _layer_norm(y, w, b):
    mu = jnp.mean(y, -1, keepdims=True)
    yc = y - mu
    var = jnp.mean(yc * yc, -1, keepdims=True)
    return yc * lax.rsqrt(var + LN_EPS) * w + b


def _proj_ln_kernel(ya_ref, yb_ref, wa_ref, wb_ref, x_ref, g_ref, b_ref, rh_ref, rm_ref, rb_ref,
                    o_ref, slab_ref, *, alpha):
    sub = min(PROJ_SUB, o_ref.shape[0])
    for i in range(o_ref.shape[0] // sub):
        rows = slice(i * sub, (i + 1) * sub)
        y = _dot(ya_ref[rows, :], wa_ref[...]) + _dot(yb_ref[rows, :], wb_ref[...]) + alpha * x_ref[rows, :]
        o_ref[rows, :] = _layer_norm(y, g_ref[...], b_ref[...])
    slab_ref[...] = _route(o_ref[...], rh_ref[...], rm_ref[...], rb_ref[...])


def _proj_ln(ya, yb, w_out, x, g, b, router, alpha, tm=512):
    t, d = x.shape
    ka = ya.shape[1]
    kb = yb.shape[1]
    tm = min(tm, t)
    wa = w_out[:ka].astype(BF16)
    wb = w_out[ka:].astype(BF16)
    row = lambda i: (i, 0)
    fix = lambda i: (0, 0)
    return pl.pallas_call(
        functools.partial(_proj_ln_kernel, alpha=alpha),
        name="proj_ln",
        grid=(t // tm,),
        in_specs=[pl.BlockSpec((tm, ka), row), pl.BlockSpec((tm, kb), row),
                  pl.BlockSpec((ka, d), fix), pl.BlockSpec((kb, d), fix),
                  pl.BlockSpec((tm, d), row), pl.BlockSpec((1, d), fix), pl.BlockSpec((1, d), fix),
                  pl.BlockSpec((d, LANES), fix), pl.BlockSpec((d, LANES), fix), pl.BlockSpec((1, LANES), fix)],
        out_specs=[pl.BlockSpec((tm, d), row), pl.BlockSpec((tm, LANES), row)],
        out_shape=[jax.ShapeDtypeStruct((t, d), F32), jax.ShapeDtypeStruct((t, LANES), F32)],
        compiler_params=_params("parallel"),
    )(ya, yb, wa, wb, x, g.reshape(1, d), b.reshape(1, d), *router)


def _route(x, wh, wm, bias):
    xh, xm, _ = _split3(x)
    d = lambda a, w: lax.dot_general(a, w, NN, preferred_element_type=F32)
    logits = d(xh, wh) + d(xh, wm) + d(xm, wh) + bias
    tm = logits.shape[0]
    lane = lax.broadcasted_iota(jnp.int32, (tm, LANES), 1)
    neg = -jnp.inf
    gl = jnp.where(lane < N_GROUPS, logits, neg)
    gmax = jnp.max(gl, -1, keepdims=True)
    g_top = jnp.min(jnp.where(gl == gmax, lane, LANES), -1, keepdims=True)
    p_group = 1.0 / jnp.sum(jnp.exp(gl - gmax), -1, keepdims=True)
    e_lane = lane - N_GROUPS
    in_group = (e_lane >= 0) & (e_lane < N_EXPERTS) & ((e_lane // EXPERTS_PER_GROUP) == g_top)
    el = jnp.where(in_group, logits, neg)
    v1 = jnp.max(el, -1, keepdims=True)
    i1 = jnp.min(jnp.where(el == v1, lane, LANES), -1, keepdims=True)
    el2 = jnp.where(lane == i1, neg, el)
    v2 = jnp.max(el2, -1, keepdims=True)
    i2 = jnp.min(jnp.where(el2 == v2, lane, LANES), -1, keepdims=True)
    e2 = jnp.exp(v2 - v1)
    w1 = 1.0 / (1.0 + e2)
    w2 = e2 / (1.0 + e2)
    return jnp.where(lane == 0, (i1 - N_GROUPS).astype(F32),
                     jnp.where(lane == 1, (i2 - N_GROUPS).astype(F32),
                               jnp.where(lane == 2, p_group * w1,
                                         jnp.where(lane == 3, p_group * w2, 0.0))))


def _router_weights(wg, bg, we, be):
    d = wg.shape[0]
    w = jnp.concatenate([wg, jnp.transpose(we, (1, 0, 2)).reshape(d, N_EXPERTS)], 1)
    w = jnp.pad(w, ((0, 0), (0, LANES - w.shape[1])))
    bias = jnp.pad(jnp.concatenate([bg, be.reshape(-1)]), (0, LANES - N_GROUPS - N_EXPERTS)).reshape(1, LANES)
    wh = w.astype(BF16)
    wm = (w - wh.astype(F32)).astype(BF16)
    return wh, wm, bias


def _ffn_kernel(bexp_ref, nvalid_ref, pk_ref, x_hbm, w1_ref, w3_ref, w2_ref, y_hbm,
                xbuf, ybuf, w1b, w3b, w2b, gsem, ssem, *, bm):
    j = pl.program_id(0)
    nb = pl.num_programs(0)
    nvalid = nvalid_ref[0]
    slot = j % 2

    @pl.when((j < nvalid) & ((j == 0) | (bexp_ref[j] != bexp_ref[jnp.maximum(j - 1, 0)])))
    def _():
        w1b[...] = w1_ref[...].astype(BF16)
        w3b[...] = w3_ref[...].astype(BF16)
        w2b[...] = w2_ref[...].astype(BF16)

    def row_copy_in(blk, s, r):
        src = pk_ref[blk * bm + r] & ((1 << TOK_BITS) - 1)
        return pltpu.make_async_copy(x_hbm.at[pl.ds(src, 1)], xbuf.at[s, pl.ds(r, 1)], gsem.at[s])

    def row_copy_out(blk, s, r):
        dst = lax.shift_right_logical(pk_ref[blk * bm + r], TOK_BITS)
        return pltpu.make_async_copy(ybuf.at[s, pl.ds(r, 1)], y_hbm.at[pl.ds(dst, 1)], ssem.at[s])

    def for_rows(fn):
        def body(r, c):
            fn(r)
            return c
        lax.fori_loop(0, bm, body, 0, unroll=8)

    @pl.when((j == 0) & (nvalid > 0))
    def _():
        for_rows(lambda r: row_copy_in(0, 0, r).start())

    @pl.when(j + 1 < nvalid)
    def _():
        for_rows(lambda r: row_copy_in(j + 1, 1 - slot, r).start())

    def wait_gather(s):
        pltpu.make_async_copy(xbuf.at[s], xbuf.at[s], gsem.at[s]).wait()

    def wait_scatter(s):
        pltpu.make_async_copy(ybuf.at[s], ybuf.at[s], ssem.at[s]).wait()

    @pl.when((j >= 2) & (j - 2 < nvalid))
    def _():
        wait_scatter(slot)

    @pl.when(j < nvalid)
    def _():
        wait_gather(slot)
        xb = xbuf[slot]
        h = _silu(_dot(xb, w1b[...])) * _dot(xb, w3b[...])
        ybuf[slot] = _dot(h, w2b[...])
        for_rows(lambda r: row_copy_out(j, slot, r).start())

    @pl.when(j >= nvalid)
    def _():
        ybuf[slot] = jnp.zeros((bm, ybuf.shape[2]), F32)
        fill = pltpu.make_async_copy(ybuf.at[slot], y_hbm.at[pl.ds(j * bm, bm)], ssem.at[slot])
        fill.start()
        fill.wait()

    @pl.when(j == nb - 1)
    def _():
        @pl.when((j >= 1) & (j - 1 < nvalid))
        def _():
            wait_scatter(1 - slot)

        @pl.when(j < nvalid)
        def _():
            wait_scatter(slot)


def _moe_dispatch(slab, t, bm):
    n = 2 * t
    e = N_EXPERTS
    rows = -(-(n + e * (bm - 1)) // bm) * bm
    ids = slab[:, :2].astype(jnp.int32).reshape(n)
    order = jnp.argsort(ids, stable=True).astype(jnp.int32)
    counts = jnp.sum((ids[:, None] == jnp.arange(e, dtype=jnp.int32)[None, :]).astype(jnp.int32), 0)
    padded = -(-counts // bm) * bm
    pend = jnp.cumsum(padded)
    pstart = pend - padded
    start = jnp.cumsum(counts) - counts
    n_blocks = rows // bm
    bstart = jnp.arange(n_blocks, dtype=jnp.int32) * bm
    block_exp = jnp.minimum(jnp.sum((bstart[:, None] >= pend[None, :]).astype(jnp.int32), 1), e - 1)
    nvalid = (pend[-1] // bm).astype(jnp.int32).reshape(1)
    per_row = lambda tbl: jnp.repeat(tbl[block_exp], bm)
    local = jnp.arange(rows, dtype=jnp.int32) - per_row(pstart)
    valid = local < per_row(counts)
    a = order[jnp.clip(per_row(start) + local, 0, n - 1)]
    pad_rank = jnp.cumsum(1 - valid.astype(jnp.int32)) - 1
    row_src = jnp.where(valid, a // 2, 0)
    row_dst = jnp.where(valid, (a % 2) * t + a // 2, n + pad_rank)
    packed = row_src | (row_dst << TOK_BITS)
    return packed.astype(jnp.int32), block_exp, nvalid, rows


def _moe_ffn(x, slab, w1, w3, w2, layer, bm=MOE_BM):
    t, d = x.shape
    de = w1.shape[3]
    assert t <= (1 << TOK_BITS)
    packed, block_exp, nvalid, rows = _moe_dispatch(slab, t, bm)
    assert rows < (1 << (32 - TOK_BITS))
    n_blocks = rows // bm
    wmap = lambda j, bexp, nv, pk: (layer, bexp[j], 0, 0)
    grid_spec = pltpu.PrefetchScalarGridSpec(
        num_scalar_prefetch=3,
        grid=(n_blocks,),
        in_specs=[pl.BlockSpec(memory_space=pl.ANY),
                  pl.BlockSpec((None, None, d, de), wmap), pl.BlockSpec((None, None, d, de), wmap),
                  pl.BlockSpec((None, None, de, d), wmap)],
        out_specs=pl.BlockSpec(memory_space=pl.ANY),
        scratch_shapes=[pltpu.VMEM((2, bm, d), F32), pltpu.VMEM((2, bm, d), F32),
                        pltpu.VMEM((d, de), BF16), pltpu.VMEM((d, de), BF16), pltpu.VMEM((de, d), BF16),
                        pltpu.SemaphoreType.DMA((2,)), pltpu.SemaphoreType.DMA((2,))],
    )
    return pl.pallas_call(
        functools.partial(_ffn_kernel, bm=bm),
        name="moe_ffn",
        grid_spec=grid_spec,
        out_shape=jax.ShapeDtypeStruct((rows, d), F32),
        compiler_params=_params("arbitrary"),
    )(block_exp, nvalid, packed, x, w1, w3, w2)


def _moe_ln_kernel(x_ref, y0_ref, y1_ref, s_ref, g_ref, b_ref, o_ref, oh_ref, *, alpha):
    s = s_ref[...]
    y = alpha * x_ref[...] + s[:, 2:3] * y0_ref[...] + s[:, 3:4] * y1_ref[...]
    o = _layer_norm(y, g_ref[...], b_ref[...])
    o_ref[...] = o
    oh_ref[...] = o.astype(BF16)


def _moe_ln(x, y, slab, g, b, alpha, tm=512):
    t, d = x.shape
    tm = min(tm, t)
    nt = t // tm
    row = lambda i: (i, 0)
    fix = lambda i: (0, 0)
    return pl.pallas_call(
        functools.partial(_moe_ln_kernel, alpha=alpha),
        name="moe_ln",
        grid=(nt,),
        in_specs=[pl.BlockSpec((tm, d), row), pl.BlockSpec((tm, d), row),
                  pl.BlockSpec((tm, d), lambda i: (i + nt, 0)), pl.BlockSpec((tm, LANES), row),
                  pl.BlockSpec((1, d), fix), pl.BlockSpec((1, d), fix)],
        out_specs=[pl.BlockSpec((tm, d), row), pl.BlockSpec((tm, d), row)],
        out_shape=[jax.ShapeDtypeStruct((t, d), F32), jax.ShapeDtypeStruct((t, d), BF16)],
        compiler_params=_params("parallel"),
    )(x, y, y, slab, g.reshape(1, d), b.reshape(1, d))


def _moe_sublayer(x, slab, w1, w3, w2, layer, g, b, alpha):
    y = _moe_ffn(x, slab, w1, w3, w2, layer)
    return _moe_ln(x, y, slab, g, b, alpha)


def _attn_kernel(q_ref, kp_ref, kc_ref, vp_ref, vc_ref, o_ref, k_all, v_all, q_all, acc_o, acc_m, acc_l, *, rows):
    n = pl.program_id(2)
    k_all[0:rows, :] = kp_ref[...]
    k_all[rows:2 * rows, :] = kc_ref[...]
    v_all[0:rows, :] = vp_ref[...]
    v_all[rows:2 * rows, :] = vc_ref[...]
    q_all[...] = q_ref[...] * (HEAD_DIM ** -0.5)
    nk = ATT_BLOCK
    qi = lax.broadcasted_iota(jnp.int32, (nk, 2 * nk), 0)
    kj = lax.broadcasted_iota(jnp.int32, (nk, 2 * nk), 1)
    band = (kj >= qi) & (kj <= qi + nk)
    lane = lax.broadcasted_iota(jnp.int32, (nk, LANES), 1)
    head1 = lane >= HEAD_DIM

    for p_idx, (window, dil) in enumerate(C_PAIRS):
        assert window // dil == nk
        span = nk * dil
        n_sub = rows // span

        def group(gi, carry, dil=dil, span=span, n_sub=n_sub, first=(p_idx == 0)):
            sel = lambda ref, start: ref[pl.ds(start, nk, stride=dil), :]
            loaded = []
            for i in range(ATT_GROUP):
                u = gi * ATT_GROUP + i
                res = u // n_sub
                sub = u - res * n_sub
                qstart = sub * span + res
                q = sel(q_all, qstart)
                k2 = jnp.concatenate([sel(k_all, rows + qstart - span), sel(k_all, rows + qstart)], 0).astype(BF16)
                v2 = jnp.concatenate([sel(v_all, rows + qstart - span), sel(v_all, rows + qstart)], 0).astype(BF16)
                dst = pl.ds(qstart, nk, stride=dil)
                old = None if first else (acc_o[dst, :], acc_m[dst, :], acc_l[dst, :])
                loaded.append((q, k2, v2, dst, old, sub))
            results = []
            for q, k2, v2, dst, old, sub in loaded:
                ok = band & ((kj >= nk) | (n > 0) | (sub > 0))
                outs = []
                for h1 in (False, True):
                    hm = head1 if h1 else jnp.logical_not(head1)
                    sc = _dot(jnp.where(hm, q, 0.0), k2, NT)
                    sc = jnp.where(ok, sc, -jnp.inf)
                    m = jnp.max(sc, -1, keepdims=True)
                    p = jnp.exp(sc - m)
                    outs.append((_dot(p, v2), m, jnp.sum(p, -1, keepdims=True)))
                o = jnp.where(head1, outs[1][0], outs[0][0])
                m = jnp.where(head1, outs[1][1], outs[0][1])
                l = jnp.where(head1, outs[1][2], outs[0][2])
                if old is not None:
                    o_old, m_old, l_old = old
                    m_new = jnp.maximum(m_old, m)
                    a_old = jnp.exp(m_old - m_new)
                    a_cur = jnp.exp(m - m_new)
                    o = o_old * a_old + o * a_cur
                    l = l_old * a_old + l * a_cur
                    m = m_new
                results.append((dst, o, m, l))
            for dst, o, m, l in results:
                acc_o[dst, :] = o
                acc_m[dst, :] = m
                acc_l[dst, :] = l
            return carry

        assert (dil * n_sub) % ATT_GROUP == 0
        lax.fori_loop(0, dil * n_sub // ATT_GROUP, group, 0)

    o_ref[...] = acc_o[...] / acc_l[...]


def _dilated_attention(p, b, s, qoff, koff, voff):
    rows = max(w for w, _ in C_PAIRS)
    assert s % rows == 0
    nblk = s // rows
    cur = lambda off: pl.BlockSpec((rows, LANES), lambda bi, hp, n: (bi * nblk + n, off + hp))
    prev = lambda off: pl.BlockSpec((rows, LANES), lambda bi, hp, n: (bi * nblk + jnp.maximum(n - 1, 0), off + hp))
    scratch = [pltpu.VMEM((2 * rows, LANES), F32), pltpu.VMEM((2 * rows, LANES), F32)] + \
              [pltpu.VMEM((rows, LANES), F32)] * 4
    return pl.pallas_call(
        functools.partial(_attn_kernel, rows=rows),
        name="dil_attn",
        grid=(b, C_WIDTH // LANES, nblk),
        in_specs=[cur(qoff), prev(koff), cur(koff), prev(voff), cur(voff)],
        out_specs=pl.BlockSpec((rows, LANES), lambda bi, hp, n: (bi * nblk + n, hp)),
        out_shape=jax.ShapeDtypeStruct((b * s, C_WIDTH), F32),
        scratch_shapes=scratch,
        compiler_params=_params("parallel", "parallel", "arbitrary"),
    )(p, p, p, p, p)


def _causal_conv(x_ref, pad_ref, w_ref, b_ref, first, blk):
    @pl.when(first)
    def _():
        pad_ref[0:8, :] = jnp.zeros((8, pad_ref.shape[1]), F32)
    pad_ref[8:8 + blk, :] = x_ref[...]
    w = w_ref[...]
    acc = b_ref[...]
    for j in range(D_CONV):
        acc = acc + w[j:j + 1, :] * pad_ref[pl.ds(8 - (D_CONV - 1) + j, blk), :]
    pad_ref[0:8, :] = pad_ref[blk:blk + 8, :]
    return acc


def _mlstm_kernel(q_ref, k_ref, v_ref, og_ref, gate_ref, cwq_ref, cwk_ref, cbq_ref, cbk_ref, gb_ref,
                  o_ref, qpad, kpad, c_state, n_state, m_state, *, blk, chunk):
    first = pl.program_id(1) == 0

    @pl.when(first)
    def _():
        c_state[...] = jnp.zeros_like(c_state)
        n_state[...] = jnp.zeros_like(n_state)
        m_state[...] = jnp.zeros_like(m_state)

    qc = _silu(_causal_conv(q_ref, qpad, cwq_ref, cbq_ref, first, blk))
    kc = _silu(_causal_conv(k_ref, kpad, cwk_ref, cbk_ref, first, blk)) * (D_DQK ** -0.5)
    gates = gate_ref[...] + gb_ref[...]
    logf = _log_sigmoid(gates)
    lane = lax.broadcasted_iota(jnp.int32, (chunk, LANES), 1)
    ti = lax.broadcasted_iota(jnp.int32, (chunk, chunk), 0)
    si = lax.broadcasted_iota(jnp.int32, (chunk, chunk), 1)
    causal = si <= ti
    tri = causal.astype(F32)
    pick = (lax.broadcasted_iota(jnp.int32, (2 * D_HEADS, LANES), 0)
            == lax.broadcasted_iota(jnp.int32, (2 * D_HEADS, LANES), 1)).astype(F32)

    c_mat = [c_state[h] for h in range(D_HEADS)]
    n_vec = [n_state[h:h + 1, :] for h in range(D_HEADS)]
    m_prev = [m_state[h:h + 1, 0:1] for h in range(D_HEADS)]
    for c in range(blk // chunk):
        rows = slice(c * chunk, (c + 1) * chunk)
        gi = gates[rows]
        fcum_all = _dot_exact_lhs(tri, logf[rows])
        f_rows = _dot_exact_lhs(pick, fcum_all, NT)
        i_rows = _dot_exact_lhs(pick, gi, NT)
        for h in range(D_HEADS):
            sel_i = lane == h
            sel_f = lane == D_HEADS + h
            qs = slice(h * D_DQK, (h + 1) * D_DQK)
            vs = slice(h * D_DV, (h + 1) * D_DV)
            q_c, k_c, v_c = qc[rows, qs], kc[rows, qs], v_ref[rows, vs]
            f_col = jnp.sum(jnp.where(sel_f, fcum_all, 0.0), -1, keepdims=True)
            i_col = jnp.sum(jnp.where(sel_i, gi, 0.0), -1, keepdims=True)
            f_row = f_rows[D_HEADS + h:D_HEADS + h + 1, :]
            i_row = i_rows[h:h + 1, :]
            log_d = jnp.where(causal, f_col - f_row + i_row, -jnp.inf)
            log_inter = f_col + m_prev[h]
            m_t = jnp.maximum(jnp.max(log_d, -1, keepdims=True), log_inter)
            w_inter = jnp.exp(log_inter - m_t)
            scores = _dot(q_c, k_c, NT) * jnp.exp(log_d - m_t)
            num = w_inter * _dot(q_c, c_mat[h]) + _dot(scores, v_c)
            den = w_inter * jnp.sum(q_c * n_vec[h], -1, keepdims=True) + jnp.sum(scores, -1, keepdims=True)
            out = num / jnp.maximum(jnp.abs(den), jnp.exp(-m_t))
            o_ref[rows, vs] = _sigmoid(og_ref[rows, vs]) * out
            f_tot = f_col[chunk - 1:chunk]
            log_w = f_tot - f_col + i_col
            m_new = jnp.maximum(f_tot + m_prev[h], jnp.max(log_w, 0, keepdims=True))
            kw = k_c * jnp.exp(log_w - m_new)
            w_old = jnp.exp(f_tot + m_prev[h] - m_new)
            c_mat[h] = w_old * c_mat[h] + _dot(kw.T, v_c)
            n_vec[h] = w_old * n_vec[h] + jnp.sum(kw, 0, keepdims=True)
            m_prev[h] = m_new
    for h in range(D_HEADS):
        c_state[h] = c_mat[h]
        n_state[h:h + 1, :] = n_vec[h]
        m_state[h:h + 1, :] = jnp.broadcast_to(m_prev[h], (1, LANES))


def _mlstm(p, b, s, off, conv_w, conv_b, b_i, b_f, blk=SEQ_BLOCK, chunk=ML_CHUNK):
    blk = min(blk, s)
    nblk = s // blk
    hq = D_QK_WIDTH // LANES
    hv = D_WIDTH // LANES
    assert off % hq == 0 and (off + 2 * hq) % hv == 0
    rowblk = lambda bi, n: bi * nblk + n
    qk_spec = lambda o: pl.BlockSpec((blk, D_QK_WIDTH), lambda bi, n: (rowblk(bi, n), off // hq + o))
    v_spec = lambda o: pl.BlockSpec((blk, D_WIDTH), lambda bi, n: (rowblk(bi, n), (off + 2 * hq) // hv + o))
    par_spec = lambda rows, o: pl.BlockSpec((rows, D_QK_WIDTH), lambda bi, n: (0, o))
    gate_off = off + 2 * hq + 2 * hv
    gbias = jnp.pad(jnp.concatenate([b_i, b_f]), (0, LANES - 2 * D_HEADS)).reshape(1, LANES)
    cb = conv_b.reshape(1, -1)
    return pl.pallas_call(
        functools.partial(_mlstm_kernel, blk=blk, chunk=chunk),
        name="mlstm",
        grid=(b, nblk),
        in_specs=[qk_spec(0), qk_spec(1), v_spec(0), v_spec(1),
                  pl.BlockSpec((blk, LANES), lambda bi, n: (rowblk(bi, n), gate_off)),
                  par_spec(D_CONV, 0), par_spec(D_CONV, 1), par_spec(1, 0), par_spec(1, 1),
                  pl.BlockSpec((1, LANES), lambda bi, n: (0, 0))],
        out_specs=pl.BlockSpec((blk, D_WIDTH), lambda bi, n: (rowblk(bi, n), 0)),
        out_shape=jax.ShapeDtypeStruct((b * s, D_WIDTH), F32),
        scratch_shapes=[pltpu.VMEM((blk + 8, D_QK_WIDTH), F32), pltpu.VMEM((blk + 8, D_QK_WIDTH), F32),
                        pltpu.VMEM((D_HEADS, D_DQK, D_DV), F32), pltpu.VMEM((8, LANES), F32),
                        pltpu.VMEM((8, LANES), F32)],
        compiler_params=_params("parallel", "arbitrary"),
    )(p, p, p, p, p, conv_w, conv_w, cb, cb, gbias)


TN = (((0,), (0,)), ((), ()))


def _chunk_masks(blk, chunk):
    t = lax.broadcasted_iota(jnp.int32, (blk, blk), 0)
    s = lax.broadcasted_iota(jnp.int32, (blk, blk), 1)
    same = (t // chunk) == (s // chunk)
    return same, same & (s <= t), same & (s < t)


def _hgrn2_kernel(q_ref, f_ref, i_ref, g_ref, lb_ref, nw_ref, o_ref, s_state, o_scr, *, blk, chunk, wide):
    @pl.when(pl.program_id(2) == 0)
    def _():
        s_state[...] = jnp.zeros_like(s_state)

    lb = lb_ref[...]
    f = f_ref[...]
    q = _silu(q_ref[...])
    v = i_ref[...]
    e = jnp.exp(-jnp.abs(f))
    a = jnp.log(lb)
    bb = jnp.log1p(-lb) + jnp.minimum(f, 0.0) - jnp.log1p(e)
    log_f = jnp.maximum(a, bb) + jnp.log1p(jnp.exp(-jnp.abs(a - bb)))
    k = (1.0 - lb) * jnp.where(f >= 0.0, e, 1.0) / (1.0 + e)

    _, tri_w, _ = _chunk_masks(blk, wide)
    bw = _dot_exact_lhs(tri_w.astype(F32), log_f)
    wchunks = [bw[c * wide:(c + 1) * wide] for c in range(blk // wide)]
    mids = [bc[wide // 2 - 1:wide // 2] for bc in wchunks]
    spread = [jnp.max(jnp.abs(bc - m)) for bc, m in zip(wchunks, mids)]
    safe = functools.reduce(jnp.maximum, spread) < GLA_SAFE_LOG

    ti = lax.broadcasted_iota(jnp.int32, (wide, wide), 0)
    si = lax.broadcasted_iota(jnp.int32, (wide, wide), 1)
    st = s_state[...]
    for c, (bc, m) in enumerate(zip(wchunks, mids)):
        rows = slice(c * wide, (c + 1) * wide)
        q_c, k_c, v_c = q[rows], k[rows], v[rows]
        total = bc[wide - 1:wide]
        att = jnp.where(si <= ti, _dot(q_c * jnp.exp(bc - m), k_c * jnp.exp(m - bc), NT), 0.0)
        o_scr[rows, :] = _dot(att, v_c) + _dot(q_c * jnp.exp(bc), st, NT)
        st = st * jnp.exp(total) + _dot(v_c, k_c * jnp.exp(total - bc), TN)

    @pl.when(safe)
    def _():
        s_state[...] = st

    @pl.when(jnp.logical_not(safe))
    def _():
        same, tri_incl, _ = _chunk_masks(blk, chunk)
        bcum = _dot_exact_lhs(tri_incl.astype(F32), log_f)
        blast = _dot_exact_lhs(same.astype(F32), log_f)
        qe = q * jnp.exp(bcum)
        kd = k * jnp.exp(blast - bcum)
        decay = jnp.exp(blast)
        pos = lax.broadcasted_iota(jnp.int32, (chunk, LANES), 0)
        st = s_state[...]
        for c in range(blk // chunk):
            rows = slice(c * chunk, (c + 1) * chunk)
            q_c, k_c, v_c, b_c = q[rows], k[rows], v[rows], bcum[rows]
            o_c = _dot(qe[rows], st, NT) + jnp.sum(q_c * k_c, -1, keepdims=True) * v_c
            for dist in range(1, chunk):
                e = jnp.exp(jnp.where(pos >= dist, b_c - pltpu.roll(b_c, dist, 0), -jnp.inf))
                w = jnp.sum(q_c * pltpu.roll(k_c, dist, 0) * e, -1, keepdims=True)
                o_c = o_c + w * pltpu.roll(v_c, dist, 0)
            o_scr[rows, :] = o_c
            st = st * decay[c * chunk:c * chunk + 1] + _dot(v_c, kd[rows], TN)
        s_state[...] = st

    o = o_scr[...]
    o = o * lax.rsqrt(jnp.mean(o * o, -1, keepdims=True) + RMS_EPS) * nw_ref[...]
    o_ref[...] = o * _silu(g_ref[...])


def _hgrn2(p, b, s, off, lb, norm_w, blk=SEQ_BLOCK, chunk=GLA_CHUNK):
    blk = min(blk, s)
    nblk = s // blk
    spec = lambda o: pl.BlockSpec((blk, LANES), lambda bi, h, n: (bi * nblk + n, off + o * B_HEADS + h))
    return pl.pallas_call(
        functools.partial(_hgrn2_kernel, blk=blk, chunk=chunk, wide=min(GLA_WIDE_CHUNK, blk)),
        name="hgrn2",
        grid=(b, B_HEADS, nblk),
        in_specs=[spec(0), spec(1), spec(2), spec(3),
                  pl.BlockSpec((1, LANES), lambda bi, h, n: (0, h)),
                  pl.BlockSpec((1, LANES), lambda bi, h, n: (0, 0))],
        out_specs=pl.BlockSpec((blk, LANES), lambda bi, h, n: (bi * nblk + n, h)),
        out_shape=jax.ShapeDtypeStruct((b * s, B_WIDTH), F32),
        scratch_shapes=[pltpu.VMEM((B_DV, B_DK), F32), pltpu.VMEM((blk, LANES), F32)],
        compiler_params=_params("parallel", "parallel", "arbitrary"),
    )(p, p, p, p, lb.reshape(1, -1), norm_w.reshape(1, -1))


def _rwkv7_kernel(r_ref, k_ref, v_ref, wa_ref, xg_ref, mur_ref, muk_ref, muv_ref, muwa_ref, mug_ref,
                  w0_ref, a0_ref, kk_ref, ka_ref, rk_ref, lnw_ref, lnb_ref, w2_ref, a2_ref, g2_ref,
                  o_ref, tail, s_state, *, blk, chunk, npp):
    @pl.when(pl.program_id(2) == 0)
    def _():
        tail[...] = jnp.zeros_like(tail)
        s_state[...] = jnp.zeros_like(s_state)

    row0 = lax.broadcasted_iota(jnp.int32, (blk, LANES), 0) == 0

    def shift_lerp(idx, x, mu):
        prev = tail[0:1, idx * LANES:(idx + 1) * LANES]
        shifted = jnp.where(row0, prev, pltpu.roll(x, 1, 0))
        tail[0:1, idx * LANES:(idx + 1) * LANES] = x[blk - 1:blk]
        return x + (shifted - x) * mu

    wa = shift_lerp(3 * npp, wa_ref[...], muwa_ref[...])
    xg = shift_lerp(3 * npp + 1, xg_ref[...], mug_ref[...])
    tanh_wa = jnp.tanh(wa)
    sig_xg = _sigmoid(xg)

    lane = lax.broadcasted_iota(jnp.int32, (LANES, LANES), 1)
    sub = lax.broadcasted_iota(jnp.int32, (LANES, LANES), 0)
    head_bd = (lane // HEAD_DIM) == (sub // HEAD_DIM)
    first_head = lax.broadcasted_iota(jnp.int32, (blk, LANES), 1) < HEAD_DIM

    def head_sum(x):
        s0 = jnp.sum(jnp.where(first_head, x, 0.0), -1, keepdims=True)
        s1 = jnp.sum(jnp.where(first_head, 0.0, x), -1, keepdims=True)
        return jnp.where(first_head, s0, s1)

    _, tri_incl, _ = _chunk_masks(blk, chunk)
    tri_incl = tri_incl.astype(F32)
    c2 = 2 * chunk
    lane_c = lax.broadcasted_iota(jnp.int32, (chunk, LANES), 1)
    h1 = lane_c >= HEAD_DIM
    h0 = jnp.logical_not(h1)
    stack = lambda x: jnp.concatenate([jnp.where(h0, x, 0.0), jnp.where(h1, x, 0.0)], 0)
    ri = lax.broadcasted_iota(jnp.int32, (2 * c2, 2 * c2), 0)
    ci = lax.broadcasted_iota(jnp.int32, (2 * c2, 2 * c2), 1)
    amask = ((ci % chunk) < (ri % chunk)) | (((ci % chunk) == (ri % chunk)) & (ri >= c2))
    eye = (lax.broadcasted_iota(jnp.int32, (c2, c2), 0) == lax.broadcasted_iota(jnp.int32, (c2, c2), 1)).astype(F32)
    nchunk = blk // chunk

    pairs = []
    for pi in range(npp):
        ls = slice(pi * LANES, (pi + 1) * LANES)
        r = shift_lerp(3 * pi, r_ref[:, ls], mur_ref[:, ls])
        k = shift_lerp(3 * pi + 1, k_ref[:, ls], muk_ref[:, ls])
        v = shift_lerp(3 * pi + 2, v_ref[:, ls], muv_ref[:, ls])
        pre_w = -(w0_ref[:, ls] + _dot(tanh_wa, w2_ref[:, ls]))
        w_log = -(jnp.maximum(pre_w, 0.0) + jnp.log1p(jnp.exp(-jnp.abs(pre_w)))) - 0.5
        lw = -jnp.exp(w_log)
        a = _sigmoid(a0_ref[:, ls] + _dot(wa, a2_ref[:, ls]))
        g = _dot(sig_xg, g2_ref[:, ls])
        kk = k * kk_ref[:, ls]
        kk = kk / jnp.maximum(jnp.sqrt(head_sum(kk * kk)), 1e-12)
        k = k * (1.0 + (a - 1.0) * ka_ref[:, ls])
        bonus = head_sum(r * k * rk_ref[:, ls]) * v
        av = -kk
        bv = kk * a
        lcum = _dot_exact_lhs(tri_incl, lw)
        lsum = jnp.concatenate([jnp.broadcast_to(lcum[(c + 1) * chunk - 1:(c + 1) * chunk], (chunk, LANES))
                                for c in range(nchunk)], 0)
        a_t = av * jnp.exp(lcum - lw)
        r_t = r * jnp.exp(lcum)
        inv = jnp.exp(-lcum)
        b_t = bv * inv
        k_t = k * inv
        to_end = jnp.exp(lsum - lcum)
        ars, amats = [], []
        for c in range(nchunk):
            rows = slice(c * chunk, (c + 1) * chunk)
            ar = jnp.concatenate([stack(a_t[rows]), stack(r_t[rows])], 0)
            bk = jnp.concatenate([stack(b_t[rows]), stack(k_t[rows])], 0)
            ars.append(ar)
            amats.append(jnp.where(amask, _dot(ar, bk, NT), 0.0))
        pairs.append(dict(v=v, g=g, bonus=bonus, b_c=bv * to_end, k_c=k * to_end, gam=jnp.exp(lsum),
                          ars=ars, amats=amats))

    keys = [(pi, c) for pi in range(npp) for c in range(nchunk)]
    xs = {key: pairs[key[0]]["amats"][key[1]][:c2, :c2] for key in keys}
    t_inv = {key: eye + xs[key] for key in keys}
    for _ in range(chunk.bit_length() - 2):
        for key in keys:
            xs[key] = _dot(xs[key], xs[key])
        for key in keys:
            t_inv[key] = t_inv[key] + _dot(t_inv[key], xs[key])

    sts = [s_state[pi] for pi in range(npp)]
    ys = [[] for _ in range(npp)]
    for c in range(nchunk):
        rows = slice(c * chunk, (c + 1) * chunk)
        for pi in range(npp):
            pr = pairs[pi]
            ar, amat = pr["ars"][c], pr["amats"][c]
            v_c = pr["v"][rows]
            v2 = jnp.concatenate([v_c, v_c], 0)
            g1 = _dot(ar, sts[pi], NT)
            u2 = _dot(t_inv[(pi, c)], g1[:c2] + _dot(amat[:c2, c2:], v2))
            y2 = g1[c2:] + _dot(amat[c2:], jnp.concatenate([u2, v2], 0))
            ys[pi].append(jnp.where(h1, y2[chunk:], y2[:chunk]))
            u = jnp.where(h1, u2[chunk:], u2[:chunk])
            upd = _dot(jnp.concatenate([u, v_c], 0), jnp.concatenate([pr["b_c"][rows], pr["k_c"][rows]], 0), TN)
            sts[pi] = jnp.where(head_bd, sts[pi] * pr["gam"][c * chunk:c * chunk + 1] + upd, 0.0)

    for pi in range(npp):
        ls = slice(pi * LANES, (pi + 1) * LANES)
        s_state[pi] = sts[pi]
        y = jnp.concatenate(ys[pi], 0)
        mu = head_sum(y) * (1.0 / HEAD_DIM)
        yc = y - mu
        var = head_sum(yc * yc) * (1.0 / HEAD_DIM)
        y = yc * lax.rsqrt(var + A_GN_EPS) * lnw_ref[:, ls] + lnb_ref[:, ls] + pairs[pi]["bonus"]
        o_ref[:, ls] = y * pairs[pi]["g"]


def _rwkv7(p, b, s, mu, w0, w2, a0, a2, g2, k_k, k_a, r_k, lnx_w, lnx_b, blk=SEQ_BLOCK, chunk=RW_CHUNK):
    blk = min(blk, s)
    nblk = s // blk
    npp = RW_PAIRS
    wide = npp * LANES
    ngrp = A_WIDTH // wide
    lora = 3 * A_WIDTH // LANES
    act = lambda o: pl.BlockSpec((blk, wide), lambda bi, hp, n: (bi * nblk + n, o * ngrp + hp))
    act_fix = lambda o: pl.BlockSpec((blk, LANES), lambda bi, hp, n: (bi * nblk + n, o))
    vec = lambda o: pl.BlockSpec((1, wide), lambda bi, hp, n: (0, o * ngrp + hp))
    vec_fix = lambda o: pl.BlockSpec((1, LANES), lambda bi, hp, n: (0, o))
    mat = pl.BlockSpec((LANES, wide), lambda bi, hp, n: (0, hp))
    row = lambda t: t.reshape(1, -1)
    w2p = jnp.concatenate([w2, jnp.zeros_like(a2)], 0)
    a2p = jnp.concatenate([jnp.zeros_like(w2), a2], 0)
    mu2 = row(mu)
    return pl.pallas_call(
        functools.partial(_rwkv7_kernel, blk=blk, chunk=chunk, npp=npp),
        name="rwkv7",
        grid=(b, ngrp, nblk),
        in_specs=[act(0), act(1), act(2), act_fix(lora), act_fix(lora + 1),
                  vec(0), vec(1), vec(2), vec_fix(lora), vec_fix(lora + 1),
                  vec(0), vec(0), vec(0), vec(0), vec(0), vec(0), vec(0), mat, mat, mat],
        out_specs=pl.BlockSpec((blk, wide), lambda bi, hp, n: (bi * nblk + n, hp)),
        out_shape=jax.ShapeDtypeStruct((b * s, A_WIDTH), F32),
        scratch_shapes=[pltpu.VMEM((8, (3 * npp + 2) * LANES), F32), pltpu.VMEM((npp, LANES, LANES), F32)],
        compiler_params=_params("parallel", "parallel", "arbitrary"),
    )(p, p, p, p, p, mu2, mu2, mu2, mu2, mu2,
      row(w0), row(a0), row(k_k), row(k_a), row(r_k), row(lnx_w), row(lnx_b), w2p, a2p, g2)


def kernel(x, ev_w_in, ev_w_out, rwkv_mu, rwkv_w0, rwkv_w2, rwkv_a0, rwkv_a2, rwkv_g2, rwkv_kk, rwkv_ka, rwkv_rk,
           rwkv_lnx_w, rwkv_lnx_b, hgrn_lb, hgrn_norm_w, od_w_in, od_w_out, mlstm_conv_w, mlstm_conv_b,
           mlstm_b_i, mlstm_b_f, ln_w, ln_b, moe_wg, moe_bg, moe_we, moe_be, moe_w1, moe_w3, moe_w2):
    b, s, d = x.shape
    depth = ln_w.shape[0]
    alpha = (2 * depth) ** 0.25
    lb_all = jnp.cumsum(jax.nn.softmax(hgrn_lb.astype(F32), axis=0), axis=0)
    lb_all = lb_all - lb_all[:1]
    tn = 2 * LANES
    xf = x.reshape(b * s, d)
    xh = xf
    for layer in range(depth):
        j = layer // 2
        if layer % 2 == 0:
            p = _matmul(xh, ev_w_in[j].astype(BF16), tn=tn)
            ya = _rwkv7(p, b, s, rwkv_mu[j], rwkv_w0[j], rwkv_w2[j], rwkv_a0[j], rwkv_a2[j], rwkv_g2[j],
                        rwkv_kk[j], rwkv_ka[j], rwkv_rk[j], rwkv_lnx_w[j], rwkv_lnx_b[j])
            yb = _hgrn2(p, b, s, A_COLS // LANES, lb_all[j], hgrn_norm_w[j])
            w_out = ev_w_out[j]
        else:
            w_in = od_w_in[j]
            w_in = jnp.pad(w_in, ((0, 0), (0, -w_in.shape[1] % tn))).astype(BF16)
            p = _matmul(xh, w_in, tn=tn)
            nq = C_WIDTH // LANES
            ya = _dilated_attention(p, b, s, 0, nq, 2 * nq)
            yb = _mlstm(p, b, s, 3 * nq, mlstm_conv_w[j], mlstm_conv_b[j], mlstm_b_i[j], mlstm_b_f[j])
            w_out = od_w_out[j]
        router = _router_weights(moe_wg[layer], moe_bg[layer], moe_we[layer], moe_be[layer])
        xf, slab = _proj_ln(ya, yb, w_out, xf, ln_w[layer, 0], ln_b[layer, 0], router, alpha)
        xf, xh = _moe_sublayer(xf, slab, moe_w1, moe_w3, moe_w2, layer, ln_w[layer, 1], ln_b[layer, 1], alpha)
    return xf.reshape(b, s, d)
```

```python
import functools

import jax
import jax.numpy as jnp
from jax import lax
from jax.experimental import pallas as pl
from jax.experimental.pallas import tpu as pltpu

F32 = jnp.float32
BF16 = jnp.bfloat16

HEAD_DIM = 64
A_HEADS = 16
A_WIDTH = A_HEADS * HEAD_DIM
A_LORA_W = 64
A_LORA_A = 64
A_LORA_G = 128
A_COLS = 3 * A_WIDTH + A_LORA_W + A_LORA_A + A_LORA_G
A_GN_EPS = 64e-5
B_HEADS = 8
B_DK = 128
B_DV = 128
B_WIDTH = B_HEADS * B_DV
C_HEADS = 16
C_WIDTH = C_HEADS * HEAD_DIM
C_PAIRS = ((128, 1), (512, 4), (2048, 16))
D_HEADS = 4
D_DQK = 128
D_DV = 256
D_QK_WIDTH = D_HEADS * D_DQK
D_WIDTH = D_HEADS * D_DV
D_CONV = 4
ODD_COLS = 3 * C_WIDTH + 2 * D_QK_WIDTH + 2 * D_WIDTH + 2 * D_HEADS
N_GROUPS = 4
EXPERTS_PER_GROUP = 8
N_EXPERTS = N_GROUPS * EXPERTS_PER_GROUP
LN_EPS = 1e-5
RMS_EPS = 1e-6

LANES = 128
VMEM_LIMIT = 56 * 1024 * 1024
RW_CHUNK = 64
RW_PAIRS = 4
GLA_WIDE_CHUNK = 64
GLA_SAFE_LOG = 60.0
GLA_CHUNK = 16
ML_CHUNK = 256
SEQ_BLOCK = 256
ATT_BLOCK = 128
ATT_GROUP = 16
IN_PROJ_TILE_BYTES = 16 * 1024 * 1024
PROJ_SUB = 128
MOE_BM = 256
TOK_BITS = 15

NN = (((1,), (0,)), ((), ()))
NT = (((1,), (1,)), ((), ()))


def _dot(a, b, dims=NN):
    return lax.dot_general(a.astype(BF16), b.astype(BF16), dims, preferred_element_type=F32)


def _split3(x):
    hi = x.astype(BF16)
    r1 = x - hi.astype(F32)
    mid = r1.astype(BF16)
    lo = (r1 - mid.astype(F32)).astype(BF16)
    return hi, mid, lo


def _dot_exact_lhs(mask, x, dims=NN):
    hi, mid, lo = _split3(x)
    m = mask.astype(BF16)
    d = lambda p: lax.dot_general(m, p, dims, preferred_element_type=F32)
    return d(hi) + d(mid) + d(lo)


def _dot_exact_rhs(x, mask):
    hi, mid, lo = _split3(x)
    m = mask.astype(BF16)
    d = lambda p: lax.dot_general(p, m, NN, preferred_element_type=F32)
    return d(hi) + d(mid) + d(lo)


def _sigmoid(x):
    return 1.0 / (1.0 + jnp.exp(-x))


def _silu(x):
    return x * _sigmoid(x)


def _log_sigmoid(x):
    return jnp.minimum(x, 0.0) - jnp.log1p(jnp.exp(-jnp.abs(x)))


def _params(*sem):
    return pltpu.CompilerParams(dimension_semantics=sem, vmem_limit_bytes=VMEM_LIMIT)


def _mm_kernel(x_ref, w_ref, o_ref):
    o_ref[...] = _dot(x_ref[...], w_ref[...])


def _matmul(x, w, tn=256):
    t, k = x.shape
    n = w.shape[1]
    tm = min(IN_PROJ_TILE_BYTES // (k * x.dtype.itemsize), t)
    return pl.pallas_call(
        _mm_kernel,
        name="in_proj",
        grid=(t // tm, n // tn),
        in_specs=[pl.BlockSpec((tm, k), lambda i, j: (i, 0)),
                  pl.BlockSpec((k, tn), lambda i, j: (0, j))],
        out_specs=pl.BlockSpec((tm, tn), lambda i, j: (i, j)),
        out_shape=jax.ShapeDtypeStruct((t, n), F32),
        compiler_params=_params("parallel", "arbitrary"),
    )(x, w)


def _layer_norm(y, w, b):
    mu = jnp.mean(y, -1, keepdims=True)
    yc = y - mu
    var = jnp.mean(yc * yc, -1, keepdims=True)
    return yc * lax.rsqrt(var + LN_EPS) * w + b


def _proj_ln_kernel(ya_ref, yb_ref, wa_ref, wb_ref, x_ref, g_ref, b_ref, rh_ref, rm_ref, rb_ref,
                    o_ref, slab_ref, *, alpha):
    sub = min(PROJ_SUB, o_ref.shape[0])
    for i in range(o_ref.shape[0] // sub):
        rows = slice(i * sub, (i + 1) * sub)
        y = _dot(ya_ref[rows, :], wa_ref[...]) + _dot(yb_ref[rows, :], wb_ref[...]) + alpha * x_ref[rows, :]
        o_ref[rows, :] = _layer_norm(y, g_ref[...], b_ref[...])
    slab_ref[...] = _route(o_ref[...], rh_ref[...], rm_ref[...], rb_ref[...])


def _proj_ln(ya, yb, w_out, x, g, b, router, alpha, tm=512):
    t, d = x.shape
    ka = ya.shape[1]
    kb = yb.shape[1]
    tm = min(tm, t)
    wa = w_out[:ka].astype(BF16)
    wb = w_out[ka:].astype(BF16)
    row = lambda i: (i, 0)
    fix = lambda i: (0, 0)
    return pl.pallas_call(
        functools.partial(_proj_ln_kernel, alpha=alpha),
        name="proj_ln",
        grid=(t // tm,),
        in_specs=[pl.BlockSpec((tm, ka), row), pl.BlockSpec((tm, kb), row),
                  pl.BlockSpec((ka, d), fix), pl.BlockSpec((kb, d), fix),
                  pl.BlockSpec((tm, d), row), pl.BlockSpec((1, d), fix), pl.BlockSpec((1, d), fix),
                  pl.BlockSpec((d, LANES), fix), pl.BlockSpec((d, LANES), fix), pl.BlockSpec((1, LANES), fix)],
        out_specs=[pl.BlockSpec((tm, d), row), pl.BlockSpec((tm, LANES), row)],
        out_shape=[jax.ShapeDtypeStruct((t, d), F32), jax.ShapeDtypeStruct((t, LANES), F32)],
        compiler_params=_params("parallel"),
    )(ya, yb, wa, wb, x, g.reshape(1, d), b.reshape(1, d), *router)


def _route(x, wh, wm, bias):
    xh, xm, _ = _split3(x)
    d = lambda a, w: lax.dot_general(a, w, NN, preferred_element_type=F32)
    logits = d(xh, wh) + d(xh, wm) + d(xm, wh) + bias
    tm = logits.shape[0]
    lane = lax.broadcasted_iota(jnp.int32, (tm, LANES), 1)
    neg = -jnp.inf
    gl = jnp.where(lane < N_GROUPS, logits, neg)
    gmax = jnp.max(gl, -1, keepdims=True)
    g_top = jnp.min(jnp.where(gl == gmax, lane, LANES), -1, keepdims=True)
    p_group = 1.0 / jnp.sum(jnp.exp(gl - gmax), -1, keepdims=True)
    e_lane = lane - N_GROUPS
    in_group = (e_lane >= 0) & (e_lane < N_EXPERTS) & ((e_lane // EXPERTS_PER_GROUP) == g_top)
    el = jnp.where(in_group, logits, neg)
    v1 = jnp.max(el, -1, keepdims=True)
    i1 = jnp.min(jnp.where(el == v1, lane, LANES), -1, keepdims=True)
    el2 = jnp.where(lane == i1, neg, el)
    v2 = jnp.max(el2, -1, keepdims=True)
    i2 = jnp.min(jnp.where(el2 == v2, lane, LANES), -1, keepdims=True)
    e2 = jnp.exp(v2 - v1)
    w1 = 1.0 / (1.0 + e2)
    w2 = e2 / (1.0 + e2)
    return jnp.where(lane == 0, (i1 - N_GROUPS).astype(F32),
                     jnp.where(lane == 1, (i2 - N_GROUPS).astype(F32),
                               jnp.where(lane == 2, p_group * w1,
                                         jnp.where(lane == 3, p_group * w2, 0.0))))


def _router_weights(wg, bg, we, be):
    d = wg.shape[0]
    w = jnp.concatenate([wg, jnp.transpose(we, (1, 0, 2)).reshape(d, N_EXPERTS)], 1)
    w = jnp.pad(w, ((0, 0), (0, LANES - w.shape[1])))
    bias = jnp.pad(jnp.concatenate([bg, be.reshape(-1)]), (0, LANES - N_GROUPS - N_EXPERTS)).reshape(1, LANES)
    wh = w.astype(BF16)
    wm = (w - wh.astype(F32)).astype(BF16)
    return wh, wm, bias


def _ffn_kernel(bexp_ref, nvalid_ref, pk_ref, x_hbm, w1_ref, w3_ref, w2_ref, y_hbm,
                xbuf, ybuf, w1b, w3b, w2b, gsem, ssem, *, bm):
    j = pl.program_id(0)
    nb = pl.num_programs(0)
    nvalid = nvalid_ref[0]
    slot = j % 2

    @pl.when((j < nvalid) & ((j == 0) | (bexp_ref[j] != bexp_ref[jnp.maximum(j - 1, 0)])))
    def _():
        w1b[...] = w1_ref[...].astype(BF16)
        w3b[...] = w3_ref[...].astype(BF16)
        w2b[...] = w2_ref[...].astype(BF16)

    def row_copy_in(blk, s, r):
        src = pk_ref[blk * bm + r] & ((1 << TOK_BITS) - 1)
        return pltpu.make_async_copy(x_hbm.at[pl.ds(src, 1)], xbuf.at[s, pl.ds(r, 1)], gsem.at[s])

    def row_copy_out(blk, s, r):
        dst = lax.shift_right_logical(pk_ref[blk * bm + r], TOK_BITS)
        return pltpu.make_async_copy(ybuf.at[s, pl.ds(r, 1)], y_hbm.at[pl.ds(dst, 1)], ssem.at[s])

    def for_rows(fn):
        def body(r, c):
            fn(r)
            return c
        lax.fori_loop(0, bm, body, 0, unroll=8)

    def for_row_pairs(fn):
        def body(i, c):
            fn(2 * i, 0)
            fn(2 * i + 1, 1)
            return c
        lax.fori_loop(0, bm // 2, body, 0, unroll=4)

    @pl.when((j == 0) & (nvalid > 0))
    def _():
        for_rows(lambda r: row_copy_in(0, 0, r).start())

    @pl.when(j + 1 < nvalid)
    def _():
        for_rows(lambda r: row_copy_in(j + 1, 1 - slot, r).start())

    def wait_gather(s):
        pltpu.make_async_copy(xbuf.at[s], xbuf.at[s], gsem.at[s]).wait()

    def wait_scatter(s):
        pltpu.make_async_copy(ybuf.at[s], ybuf.at[s], ssem.at[s]).wait()

    @pl.when((j >= 2) & (j - 2 < nvalid))
    def _():
        wait_scatter(slot)

    @pl.when(j < nvalid)
    def _():
        wait_gather(slot)
        xb = xbuf[slot]
        h = _silu(_dot(xb, w1b[...])) * _dot(xb, w3b[...])
        ybuf[slot] = _dot(h, w2b[...])
        for_row_pairs(lambda r, pri: row_copy_out(j, slot, r).start(priority=pri))

    @pl.when(j >= nvalid)
    def _():
        ybuf[slot] = jnp.zeros((bm, ybuf.shape[2]), F32)
        fill = pltpu.make_async_copy(ybuf.at[slot], y_hbm.at[pl.ds(j * bm, bm)], ssem.at[slot])
        fill.start()
        fill.wait()

    @pl.when(j == nb - 1)
    def _():
        @pl.when((j >= 1) & (j - 1 < nvalid))
        def _():
            wait_scatter(1 - slot)

        @pl.when(j < nvalid)
        def _():
            wait_scatter(slot)


def _moe_dispatch(slab, t, bm):
    n = 2 * t
    e = N_EXPERTS
    rows = -(-(n + e * (bm - 1)) // bm) * bm
    ids = slab[:, :2].astype(jnp.int32).reshape(n)
    order = jnp.argsort(ids, stable=True).astype(jnp.int32)
    counts = jnp.sum((ids[:, None] == jnp.arange(e, dtype=jnp.int32)[None, :]).astype(jnp.int32), 0)
    padded = -(-counts // bm) * bm
    pend = jnp.cumsum(padded)
    pstart = pend - padded
    start = jnp.cumsum(counts) - counts
    n_blocks = rows // bm
    bstart = jnp.arange(n_blocks, dtype=jnp.int32) * bm
    block_exp = jnp.minimum(jnp.sum((bstart[:, None] >= pend[None, :]).astype(jnp.int32), 1), e - 1)
    nvalid = (pend[-1] // bm).astype(jnp.int32).reshape(1)
    per_row = lambda tbl: jnp.repeat(tbl[block_exp], bm)
    local = jnp.arange(rows, dtype=jnp.int32) - per_row(pstart)
    valid = local < per_row(counts)
    a = order[jnp.clip(per_row(start) + local, 0, n - 1)]
    pad_rank = jnp.cumsum(1 - valid.astype(jnp.int32)) - 1
    row_src = jnp.where(valid, a // 2, 0)
    row_dst = jnp.where(valid, (a % 2) * t + a // 2, n + pad_rank)
    packed = row_src | (row_dst << TOK_BITS)
    return packed.astype(jnp.int32), block_exp, nvalid, rows


def _moe_ffn(x, slab, w1, w3, w2, layer, bm=MOE_BM):
    t, d = x.shape
    de = w1.shape[3]
    assert t <= (1 << TOK_BITS)
    packed, block_exp, nvalid, rows = _moe_dispatch(slab, t, bm)
    assert rows < (1 << (32 - TOK_BITS))
    n_blocks = rows // bm
    wmap = lambda j, bexp, nv, pk: (layer, bexp[j], 0, 0)
    grid_spec = pltpu.PrefetchScalarGridSpec(
        num_scalar_prefetch=3,
        grid=(n_blocks,),
        in_specs=[pl.BlockSpec(memory_space=pl.ANY),
                  pl.BlockSpec((None, None, d, de), wmap), pl.BlockSpec((None, None, d, de), wmap),
                  pl.BlockSpec((None, None, de, d), wmap)],
        out_specs=pl.BlockSpec(memory_space=pl.ANY),
        scratch_shapes=[pltpu.VMEM((2, bm, d), F32), pltpu.VMEM((2, bm, d), F32),
                        pltpu.VMEM((d, de), BF16), pltpu.VMEM((d, de), BF16), pltpu.VMEM((de, d), BF16),
                        pltpu.SemaphoreType.DMA((2,)), pltpu.SemaphoreType.DMA((2,))],
    )
    return pl.pallas_call(
        functools.partial(_ffn_kernel, bm=bm),
        name="moe_ffn",
        grid_spec=grid_spec,
        out_shape=jax.ShapeDtypeStruct((rows, d), F32),
        compiler_params=_params("arbitrary"),
    )(block_exp, nvalid, packed, x, w1, w3, w2)


def _moe_ln_kernel(x_ref, y0_ref, y1_ref, s_ref, g_ref, b_ref, o_ref, oh_ref, *, alpha):
    s = s_ref[...]
    y = alpha * x_ref[...] + s[:, 2:3] * y0_ref[...] + s[:, 3:4] * y1_ref[...]
    o = _layer_norm(y, g_ref[...], b_ref[...])
    o_ref[...] = o
    oh_ref[...] = o.astype(BF16)


def _moe_ln(x, y, slab, g, b, alpha, tm=512):
    t, d = x.shape
    tm = min(tm, t)
    nt = t // tm
    row = lambda i: (i, 0)
    fix = lambda i: (0, 0)
    return pl.pallas_call(
        functools.partial(_moe_ln_kernel, alpha=alpha),
        name="moe_ln",
        grid=(nt,),
        in_specs=[pl.BlockSpec((tm, d), row), pl.BlockSpec((tm, d), row),
                  pl.BlockSpec((tm, d), lambda i: (i + nt, 0)), pl.BlockSpec((tm, LANES), row),
                  pl.BlockSpec((1, d), fix), pl.BlockSpec((1, d), fix)],
        out_specs=[pl.BlockSpec((tm, d), row), pl.BlockSpec((tm, d), row)],
        out_shape=[jax.ShapeDtypeStruct((t, d), F32), jax.ShapeDtypeStruct((t, d), BF16)],
        compiler_params=_params("parallel"),
    )(x, y, y, slab, g.reshape(1, d), b.reshape(1, d))


def _moe_sublayer(x, slab, w1, w3, w2, layer, g, b, alpha):
    y = _moe_ffn(x, slab, w1, w3, w2, layer)
    return _moe_ln(x, y, slab, g, b, alpha)


def _attn_kernel(q_ref, kp_ref, kc_ref, vp_ref, vc_ref, o_ref, k_all, v_all, q_all, acc_o, acc_m, acc_l, *, rows):
    n = pl.program_id(2)
    k_all[0:rows, :] = kp_ref[...]
    k_all[rows:2 * rows, :] = kc_ref[...]
    v_all[0:rows, :] = vp_ref[...]
    v_all[rows:2 * rows, :] = vc_ref[...]
    q_all[...] = q_ref[...] * (HEAD_DIM ** -0.5)
    nk = ATT_BLOCK
    qi = lax.broadcasted_iota(jnp.int32, (nk, 2 * nk), 0)
    kj = lax.broadcasted_iota(jnp.int32, (nk, 2 * nk), 1)
    band = (kj >= qi) & (kj <= qi + nk)
    lane = lax.broadcasted_iota(jnp.int32, (nk, LANES), 1)
    head1 = lane >= HEAD_DIM

    for p_idx, (window, dil) in enumerate(C_PAIRS):
        assert window // dil == nk
        span = nk * dil
        n_sub = rows // span

        def group(gi, carry, dil=dil, span=span, n_sub=n_sub, first=(p_idx == 0)):
            sel = lambda ref, start: ref[pl.ds(start, nk, stride=dil), :]
            loaded = []
            for i in range(ATT_GROUP):
                u = gi * ATT_GROUP + i
                res = u // n_sub
                sub = u - res * n_sub
                qstart = sub * span + res
                q = sel(q_all, qstart)
                k2 = jnp.concatenate([sel(k_all, rows + qstart - span), sel(k_all, rows + qstart)], 0).astype(BF16)
                v2 = jnp.concatenate([sel(v_all, rows + qstart - span), sel(v_all, rows + qstart)], 0).astype(BF16)
                dst = pl.ds(qstart, nk, stride=dil)
                old = None if first else (acc_o[dst, :], acc_m[dst, :], acc_l[dst, :])
                loaded.append((q, k2, v2, dst, old, sub))
            results = []
            for q, k2, v2, dst, old, sub in loaded:
                ok = band & ((kj >= nk) | (n > 0) | (sub > 0))
                outs = []
                for h1 in (False, True):
                    hm = head1 if h1 else jnp.logical_not(head1)
                    sc = _dot(jnp.where(hm, q, 0.0), k2, NT)
                    sc = jnp.where(ok, sc, -jnp.inf)
                    m = jnp.max(sc, -1, keepdims=True)
                    p = jnp.exp(sc - m)
                    outs.append((_dot(p, v2), m, jnp.sum(p, -1, keepdims=True)))
                o = jnp.where(head1, outs[1][0], outs[0][0])
                m = jnp.where(head1, outs[1][1], outs[0][1])
                l = jnp.where(head1, outs[1][2], outs[0][2])
                if old is not None:
                    o_old, m_old, l_old = old
                    m_new = jnp.maximum(m_old, m)
                    a_old = jnp.exp(m_old - m_new)
                    a_cur = jnp.exp(m - m_new)
                    o = o_old * a_old + o * a_cur
                    l = l_old * a_old + l * a_cur
                    m = m_new
                results.append((dst, o, m, l))
            for dst, o, m, l in results:
                acc_o[dst, :] = o
                acc_m[dst, :] = m
                acc_l[dst, :] = l
            return carry

        assert (dil * n_sub) % ATT_GROUP == 0
        lax.fori_loop(0, dil * n_sub // ATT_GROUP, group, 0)

    o_ref[...] = acc_o[...] / acc_l[...]


def _dilated_attention(p, b, s, qoff, koff, voff):
    rows = max(w for w, _ in C_PAIRS)
    assert s % rows == 0
    nblk = s // rows
    cur = lambda off: pl.BlockSpec((rows, LANES), lambda bi, hp, n: (bi * nblk + n, off + hp))
    prev = lambda off: pl.BlockSpec((rows, LANES), lambda bi, hp, n: (bi * nblk + jnp.maximum(n - 1, 0), off + hp))
    scratch = [pltpu.VMEM((2 * rows, LANES), F32), pltpu.VMEM((2 * rows, LANES), F32)] + \
              [pltpu.VMEM((rows, LANES), F32)] * 4
    return pl.pallas_call(
        functools.partial(_attn_kernel, rows=rows),
        name="dil_attn",
        grid=(b, C_WIDTH // LANES, nblk),
        in_specs=[cur(qoff), prev(koff), cur(koff), prev(voff), cur(voff)],
        out_specs=pl.BlockSpec((rows, LANES), lambda bi, hp, n: (bi * nblk + n, hp)),
        out_shape=jax.ShapeDtypeStruct((b * s, C_WIDTH), F32),
        scratch_shapes=scratch,
        compiler_params=_params("parallel", "parallel", "arbitrary"),
    )(p, p, p, p, p)


def _causal_conv(x_ref, pad_ref, w_ref, b_ref, first, blk):
    @pl.when(first)
    def _():
        pad_ref[0:8, :] = jnp.zeros((8, pad_ref.shape[1]), F32)
    pad_ref[8:8 + blk, :] = x_ref[...]
    w = w_ref[...]
    acc = b_ref[...]
    for j in range(D_CONV):
        acc = acc + w[j:j + 1, :] * pad_ref[pl.ds(8 - (D_CONV - 1) + j, blk), :]
    pad_ref[0:8, :] = pad_ref[blk:blk + 8, :]
    return acc


def _mlstm_kernel(q_ref, k_ref, v_ref, og_ref, gate_ref, cwq_ref, cwk_ref, cbq_ref, cbk_ref, gb_ref,
                  o_ref, qpad, kpad, c_state, n_state, m_state, *, blk, chunk):
    first = pl.program_id(1) == 0

    @pl.when(first)
    def _():
        c_state[...] = jnp.zeros_like(c_state)
        n_state[...] = jnp.zeros_like(n_state)
        m_state[...] = jnp.zeros_like(m_state)

    qc = _silu(_causal_conv(q_ref, qpad, cwq_ref, cbq_ref, first, blk))
    kc = _silu(_causal_conv(k_ref, kpad, cwk_ref, cbk_ref, first, blk)) * (D_DQK ** -0.5)
    gates = gate_ref[...] + gb_ref[...]
    logf = _log_sigmoid(gates)
    lane = lax.broadcasted_iota(jnp.int32, (chunk, LANES), 1)
    ti = lax.broadcasted_iota(jnp.int32, (chunk, chunk), 0)
    si = lax.broadcasted_iota(jnp.int32, (chunk, chunk), 1)
    causal = si <= ti
    tri = causal.astype(F32)
    pick = (lax.broadcasted_iota(jnp.int32, (2 * D_HEADS, LANES), 0)
            == lax.broadcasted_iota(jnp.int32, (2 * D_HEADS, LANES), 1)).astype(F32)

    c_mat = [c_state[h] for h in range(D_HEADS)]
    n_vec = [n_state[h:h + 1, :] for h in range(D_HEADS)]
    m_prev = [m_state[h:h + 1, 0:1] for h in range(D_HEADS)]
    for c in range(blk // chunk):
        rows = slice(c * chunk, (c + 1) * chunk)
        gi = gates[rows]
        fcum_all = _dot_exact_lhs(tri, logf[rows])
        f_rows = _dot_exact_lhs(pick, fcum_all, NT)
        i_rows = _dot_exact_lhs(pick, gi, NT)
        for h in range(D_HEADS):
            sel_i = lane == h
            sel_f = lane == D_HEADS + h
            qs = slice(h * D_DQK, (h + 1) * D_DQK)
            vs = slice(h * D_DV, (h + 1) * D_DV)
            q_c, k_c, v_c = qc[rows, qs], kc[rows, qs], v_ref[rows, vs]
            f_col = jnp.sum(jnp.where(sel_f, fcum_all, 0.0), -1, keepdims=True)
            i_col = jnp.sum(jnp.where(sel_i, gi, 0.0), -1, keepdims=True)
            f_row = f_rows[D_HEADS + h:D_HEADS + h + 1, :]
            i_row = i_rows[h:h + 1, :]
            log_d = jnp.where(causal, f_col - f_row + i_row, -jnp.inf)
            log_inter = f_col + m_prev[h]
            m_t = jnp.maximum(jnp.max(log_d, -1, keepdims=True), log_inter)
            w_inter = jnp.exp(log_inter - m_t)
            scores = _dot(q_c, k_c, NT) * jnp.exp(log_d - m_t)
            num = w_inter * _dot(q_c, c_mat[h]) + _dot(scores, v_c)
            den = w_inter * jnp.sum(q_c * n_vec[h], -1, keepdims=True) + jnp.sum(scores, -1, keepdims=True)
            out = num / jnp.maximum(jnp.abs(den), jnp.exp(-m_t))
            o_ref[rows, vs] = _sigmoid(og_ref[rows, vs]) * out
            f_tot = f_col[chunk - 1:chunk]
            log_w = f_tot - f_col + i_col
            m_new = jnp.maximum(f_tot + m_prev[h], jnp.max(log_w, 0, keepdims=True))
            kw = k_c * jnp.exp(log_w - m_new)
            w_old = jnp.exp(f_tot + m_prev[h] - m_new)
            c_mat[h] = w_old * c_mat[h] + _dot(kw.T, v_c)
            n_vec[h] = w_old * n_vec[h] + jnp.sum(kw, 0, keepdims=True)
            m_prev[h] = m_new
    for h in range(D_HEADS):
        c_state[h] = c_mat[h]
        n_state[h:h + 1, :] = n_vec[h]
        m_state[h:h + 1, :] = jnp.broadcast_to(m_prev[h], (1, LANES))


def _mlstm(p, b, s, off, conv_w, conv_b, b_i, b_f, blk=SEQ_BLOCK, chunk=ML_CHUNK):
    blk = min(blk, s)
    nblk = s // blk
    hq = D_QK_WIDTH // LANES
    hv = D_WIDTH // LANES
    assert off % hq == 0 and (off + 2 * hq) % hv == 0
    rowblk = lambda bi, n: bi * nblk + n
    qk_spec = lambda o: pl.BlockSpec((blk, D_QK_WIDTH), lambda bi, n: (rowblk(bi, n), off // hq + o))
    v_spec = lambda o: pl.BlockSpec((blk, D_WIDTH), lambda bi, n: (rowblk(bi, n), (off + 2 * hq) // hv + o))
    par_spec = lambda rows, o: pl.BlockSpec((rows, D_QK_WIDTH), lambda bi, n: (0, o))
    gate_off = off + 2 * hq + 2 * hv
    gbias = jnp.pad(jnp.concatenate([b_i, b_f]), (0, LANES - 2 * D_HEADS)).reshape(1, LANES)
    cb = conv_b.reshape(1, -1)
    return pl.pallas_call(
        functools.partial(_mlstm_kernel, blk=blk, chunk=chunk),
        name="mlstm",
        grid=(b, nblk),
        in_specs=[qk_spec(0), qk_spec(1), v_spec(0), v_spec(1),
                  pl.BlockSpec((blk, LANES), lambda bi, n: (rowblk(bi, n), gate_off)),
                  par_spec(D_CONV, 0), par_spec(D_CONV, 1), par_spec(1, 0), par_spec(1, 1),
                  pl.BlockSpec((1, LANES), lambda bi, n: (0, 0))],
        out_specs=pl.BlockSpec((blk, D_WIDTH), lambda bi, n: (rowblk(bi, n), 0)),
        out_shape=jax.ShapeDtypeStruct((b * s, D_WIDTH), F32),
        scratch_shapes=[pltpu.VMEM((blk + 8, D_QK_WIDTH), F32), pltpu.VMEM((blk + 8, D_QK_WIDTH), F32),
                        pltpu.VMEM((D_HEADS, D_DQK, D_DV), F32), pltpu.VMEM((8, LANES), F32),
                        pltpu.VMEM((8, LANES), F32)],
        compiler_params=_params("parallel", "arbitrary"),
    )(p, p, p, p, p, conv_w, conv_w, cb, cb, gbias)


TN = (((0,), (0,)), ((), ()))


def _chunk_masks(blk, chunk):
    t = lax.broadcasted_iota(jnp.int32, (blk, blk), 0)
    s = lax.broadcasted_iota(jnp.int32, (blk, blk), 1)
    same = (t // chunk) == (s // chunk)
    return same, same & (s <= t), same & (s < t)


def _hgrn2_kernel(q_ref, f_ref, i_ref, g_ref, lb_ref, nw_ref, o_ref, s_state, o_scr, *, blk, chunk, wide):
    @pl.when(pl.program_id(2) == 0)
    def _():
        s_state[...] = jnp.zeros_like(s_state)

    lb = lb_ref[...]
    f = f_ref[...]
    q = _silu(q_ref[...])
    v = i_ref[...]
    e = jnp.exp(-jnp.abs(f))
    a = jnp.log(lb)
    bb = jnp.log1p(-lb) + jnp.minimum(f, 0.0) - jnp.log1p(e)
    log_f = jnp.maximum(a, bb) + jnp.log1p(jnp.exp(-jnp.abs(a - bb)))
    k = (1.0 - lb) * jnp.where(f >= 0.0, e, 1.0) / (1.0 + e)

    _, tri_w, _ = _chunk_masks(blk, wide)
    bw = _dot_exact_lhs(tri_w.astype(F32), log_f)
    wchunks = [bw[c * wide:(c + 1) * wide] for c in range(blk // wide)]
    mids = [bc[wide // 2 - 1:wide // 2] for bc in wchunks]
    spread = [jnp.max(jnp.abs(bc - m)) for bc, m in zip(wchunks, mids)]
    safe = functools.reduce(jnp.maximum, spread) < GLA_SAFE_LOG

    ti = lax.broadcasted_iota(jnp.int32, (wide, wide), 0)
    si = lax.broadcasted_iota(jnp.int32, (wide, wide), 1)
    st = s_state[...]
    for c, (bc, m) in enumerate(zip(wchunks, mids)):
        rows = slice(c * wide, (c + 1) * wide)
        q_c, k_c, v_c = q[rows], k[rows], v[rows]
        total = bc[wide - 1:wide]
        att = jnp.where(si <= ti, _dot(q_c * jnp.exp(bc - m), k_c * jnp.exp(m - bc), NT), 0.0)
        o_scr[rows, :] = _dot(att, v_c) + _dot(q_c * jnp.exp(bc), st, NT)
        st = st * jnp.exp(total) + _dot(v_c, k_c * jnp.exp(total - bc), TN)

    @pl.when(safe)
    def _():
        s_state[...] = st

    @pl.when(jnp.logical_not(safe))
    def _():
        same, tri_incl, _ = _chunk_masks(blk, chunk)
        bcum = _dot_exact_lhs(tri_incl.astype(F32), log_f)
        blast = _dot_exact_lhs(same.astype(F32), log_f)
        qe = q * jnp.exp(bcum)
        kd = k * jnp.exp(blast - bcum)
        decay = jnp.exp(blast)
        pos = lax.broadcasted_iota(jnp.int32, (chunk, LANES), 0)
        st = s_state[...]
        for c in range(blk // chunk):
            rows = slice(c * chunk, (c + 1) * chunk)
            q_c, k_c, v_c, b_c = q[rows], k[rows], v[rows], bcum[rows]
            o_c = _dot(qe[rows], st, NT) + jnp.sum(q_c * k_c, -1, keepdims=True) * v_c
            for dist in range(1, chunk):
                e = jnp.exp(jnp.where(pos >= dist, b_c - pltpu.roll(b_c, dist, 0), -jnp.inf))
                w = jnp.sum(q_c * pltpu.roll(k_c, dist, 0) * e, -1, keepdims=True)
                o_c = o_c + w * pltpu.roll(v_c, dist, 0)
            o_scr[rows, :] = o_c
            st = st * decay[c * chunk:c * chunk + 1] + _dot(v_c, kd[rows], TN)
        s_state[...] = st

    o = o_scr[...]
    o = o * lax.rsqrt(jnp.mean(o * o, -1, keepdims=True) + RMS_EPS) * nw_ref[...]
    o_ref[...] = o * _silu(g_ref[...])


def _hgrn2(p, b, s, off, lb, norm_w, blk=SEQ_BLOCK, chunk=GLA_CHUNK):
    blk = min(blk, s)
    nblk = s // blk
    spec = lambda o: pl.BlockSpec((blk, LANES), lambda bi, h, n: (bi * nblk + n, off + o * B_HEADS + h))
    return pl.pallas_call(
        functools.partial(_hgrn2_kernel, blk=blk, chunk=chunk, wide=min(GLA_WIDE_CHUNK, blk)),
        name="hgrn2",
        grid=(b, B_HEADS, nblk),
        in_specs=[spec(0), spec(1), spec(2), spec(3),
                  pl.BlockSpec((1, LANES), lambda bi, h, n: (0, h)),
                  pl.BlockSpec((1, LANES), lambda bi, h, n: (0, 0))],
        out_specs=pl.BlockSpec((blk, LANES), lambda bi, h, n: (bi * nblk + n, h)),
        out_shape=jax.ShapeDtypeStruct((b * s, B_WIDTH), F32),
        scratch_shapes=[pltpu.VMEM((B_DV, B_DK), F32), pltpu.VMEM((blk, LANES), F32)],
        compiler_params=_params("parallel", "parallel", "arbitrary"),
    )(p, p, p, p, lb.reshape(1, -1), norm_w.reshape(1, -1))


def _rwkv7_kernel(r_ref, k_ref, v_ref, wa_ref, xg_ref, mur_ref, muk_ref, muv_ref, muwa_ref, mug_ref,
                  w0_ref, a0_ref, kk_ref, ka_ref, rk_ref, lnw_ref, lnb_ref, w2_ref, a2_ref, g2_ref,
                  o_ref, tail, s_state, *, blk, chunk, npp):
    @pl.when(pl.program_id(2) == 0)
    def _():
        tail[...] = jnp.zeros_like(tail)
        s_state[...] = jnp.zeros_like(s_state)

    row0 = lax.broadcasted_iota(jnp.int32, (blk, LANES), 0) == 0

    def shift_lerp(idx, x, mu):
        prev = tail[0:1, idx * LANES:(idx + 1) * LANES]
        shifted = jnp.where(row0, prev, pltpu.roll(x, 1, 0))
        tail[0:1, idx * LANES:(idx + 1) * LANES] = x[blk - 1:blk]
        return x + (shifted - x) * mu

    wa = shift_lerp(3 * npp, wa_ref[...], muwa_ref[...])
    xg = shift_lerp(3 * npp + 1, xg_ref[...], mug_ref[...])
    tanh_wa = jnp.tanh(wa)
    sig_xg = _sigmoid(xg)

    lane = lax.broadcasted_iota(jnp.int32, (LANES, LANES), 1)
    sub = lax.broadcasted_iota(jnp.int32, (LANES, LANES), 0)
    head_bd = (lane // HEAD_DIM) == (sub // HEAD_DIM)
    first_head = lax.broadcasted_iota(jnp.int32, (blk, LANES), 1) < HEAD_DIM

    def head_sum(x):
        s0 = jnp.sum(jnp.where(first_head, x, 0.0), -1, keepdims=True)
        s1 = jnp.sum(jnp.where(first_head, 0.0, x), -1, keepdims=True)
        return jnp.where(first_head, s0, s1)

    _, tri_incl, _ = _chunk_masks(blk, chunk)
    tri_incl = tri_incl.astype(F32)
    c2 = 2 * chunk
    lane_c = lax.broadcasted_iota(jnp.int32, (chunk, LANES), 1)
    h1 = lane_c >= HEAD_DIM
    h0 = jnp.logical_not(h1)
    stack = lambda x: jnp.concatenate([jnp.where(h0, x, 0.0), jnp.where(h1, x, 0.0)], 0)
    ri = lax.broadcasted_iota(jnp.int32, (2 * c2, 2 * c2), 0)
    ci = lax.broadcasted_iota(jnp.int32, (2 * c2, 2 * c2), 1)
    amask = ((ci % chunk) < (ri % chunk)) | (((ci % chunk) == (ri % chunk)) & (ri >= c2))
    eye = (lax.broadcasted_iota(jnp.int32, (c2, c2), 0) == lax.broadcasted_iota(jnp.int32, (c2, c2), 1)).astype(F32)
    nchunk = blk // chunk

    pairs = []
    for pi in range(npp):
        ls = slice(pi * LANES, (pi + 1) * LANES)
        r = shift_lerp(3 * pi, r_ref[:, ls], mur_ref[:, ls])
        k = shift_lerp(3 * pi + 1, k_ref[:, ls], muk_ref[:, ls])
        v = shift_lerp(3 * pi + 2, v_ref[:, ls], muv_ref[:, ls])
        pre_w = -(w0_ref[:, ls] + _dot(tanh_wa, w2_ref[:, ls]))
        w_log = -(jnp.maximum(pre_w, 0.0) + jnp.log1p(jnp.exp(-jnp.abs(pre_w)))) - 0.5
        lw = -jnp.exp(w_log)
        a = _sigmoid(a0_ref[:, ls] + _dot(wa, a2_ref[:, ls]))
        g = _dot(sig_xg, g2_ref[:, ls])
        kk = k * kk_ref[:, ls]
        kk = kk / jnp.maximum(jnp.sqrt(head_sum(kk * kk)), 1e-12)
        k = k * (1.0 + (a - 1.0) * ka_ref[:, ls])
        bonus = head_sum(r * k * rk_ref[:, ls]) * v
        av = -kk
        bv = kk * a
        lcum = _dot_exact_lhs(tri_incl, lw)
        lsum = jnp.concatenate([jnp.broadcast_to(lcum[(c + 1) * chunk - 1:(c + 1) * chunk], (chunk, LANES))
                                for c in range(nchunk)], 0)
        a_t = av * jnp.exp(lcum - lw)
        r_t = r * jnp.exp(lcum)
        inv = jnp.exp(-lcum)
        b_t = bv * inv
        k_t = k * inv
        to_end = jnp.exp(lsum - lcum)
        ars, amats = [], []
        for c in range(nchunk):
            rows = slice(c * chunk, (c + 1) * chunk)
            ar = jnp.concatenate([stack(a_t[rows]), stack(r_t[rows])], 0)
            bk = jnp.concatenate([stack(b_t[rows]), stack(k_t[rows])], 0)
            ars.append(ar)
            amats.append(jnp.where(amask, _dot(ar, bk, NT), 0.0))
        pairs.append(dict(v=v, g=g, bonus=bonus, b_c=bv * to_end, k_c=k * to_end, gam=jnp.exp(lsum),
                          ars=ars, amats=amats))

    keys = [(pi, c) for pi in range(npp) for c in range(nchunk)]
    xs = {key: pairs[key[0]]["amats"][key[1]][:c2, :c2] for key in keys}
    t_inv = {key: eye + xs[key] for key in keys}
    for _ in range(chunk.bit_length() - 2):
        for key in keys:
            xs[key] = _dot(xs[key], xs[key])
        for key in keys:
            t_inv[key] = t_inv[key] + _dot(t_inv[key], xs[key])

    sts = [s_state[pi] for pi in range(npp)]
    ys = [[] for _ in range(npp)]
    for c in range(nchunk):
        rows = slice(c * chunk, (c + 1) * chunk)
        for pi in range(npp):
            pr = pairs[pi]
            ar, amat = pr["ars"][c], pr["amats"][c]
            v_c = pr["v"][rows]
            v2 = jnp.concatenate([v_c, v_c], 0)
            g1 = _dot(ar, sts[pi], NT)
            u2 = _dot(t_inv[(pi, c)], g1[:c2] + _dot(amat[:c2, c2:], v2))
            y2 = g1[c2:] + _dot(amat[c2:], jnp.concatenate([u2, v2], 0))
            ys[pi].append(jnp.where(h1, y2[chunk:], y2[:chunk]))
            u = jnp.where(h1, u2[chunk:], u2[:chunk])
            upd = _dot(jnp.concatenate([u, v_c], 0), jnp.concatenate([pr["b_c"][rows], pr["k_c"][rows]], 0), TN)
            sts[pi] = jnp.where(head_bd, sts[pi] * pr["gam"][c * chunk:c * chunk + 1] + upd, 0.0)

    for pi in range(npp):
        ls = slice(pi * LANES, (pi + 1) * LANES)
        s_state[pi] = sts[pi]
        y = jnp.concatenate(ys[pi], 0)
        mu = head_sum(y) * (1.0 / HEAD_DIM)
        yc = y - mu
        var = head_sum(yc * yc) * (1.0 / HEAD_DIM)
        y = yc * lax.rsqrt(var + A_GN_EPS) * lnw_ref[:, ls] + lnb_ref[:, ls] + pairs[pi]["bonus"]
        o_ref[:, ls] = y * pairs[pi]["g"]


def _rwkv7(p, b, s, mu, w0, w2, a0, a2, g2, k_k, k_a, r_k, lnx_w, lnx_b, blk=SEQ_BLOCK, chunk=RW_CHUNK):
    blk = min(blk, s)
    nblk = s // blk
    npp = RW_PAIRS
    wide = npp * LANES
    ngrp = A_WIDTH // wide
    lora = 3 * A_WIDTH // LANES
    act = lambda o: pl.BlockSpec((blk, wide), lambda bi, hp, n: (bi * nblk + n, o * ngrp + hp))
    act_fix = lambda o: pl.BlockSpec((blk, LANES), lambda bi, hp, n: (bi * nblk + n, o))
    vec = lambda o: pl.BlockSpec((1, wide), lambda bi, hp, n: (0, o * ngrp + hp))
    vec_fix = lambda o: pl.BlockSpec((1, LANES), lambda bi, hp, n: (0, o))
    mat = pl.BlockSpec((LANES, wide), lambda bi, hp, n: (0, hp))
    row = lambda t: t.reshape(1, -1)
    w2p = jnp.concatenate([w2, jnp.zeros_like(a2)], 0)
    a2p = jnp.concatenate([jnp.zeros_like(w2), a2], 0)
    mu2 = row(mu)
    return pl.pallas_call(
        functools.partial(_rwkv7_kernel, blk=blk, chunk=chunk, npp=npp),
        name="rwkv7",
        grid=(b, ngrp, nblk),
        in_specs=[act(0), act(1), act(2), act_fix(lora), act_fix(lora + 1),
                  vec(0), vec(1), vec(2), vec_fix(lora), vec_fix(lora + 1),
                  vec(0), vec(0), vec(0), vec(0), vec(0), vec(0), vec(0), mat, mat, mat],
        out_specs=pl.BlockSpec((blk, wide), lambda bi, hp, n: (bi * nblk + n, hp)),
        out_shape=jax.ShapeDtypeStruct((b * s, A_WIDTH), F32),
        scratch_shapes=[pltpu.VMEM((8, (3 * npp + 2) * LANES), F32), pltpu.VMEM((npp, LANES, LANES), F32)],
        compiler_params=_params("parallel", "parallel", "arbitrary"),
    )(p, p, p, p, p, mu2, mu2, mu2, mu2, mu2,
      row(w0), row(a0), row(k_k), row(k_a), row(r_k), row(lnx_w), row(lnx_b), w2p, a2p, g2)


def kernel(x, ev_w_in, ev_w_out, rwkv_mu, rwkv_w0, rwkv_w2, rwkv_a0, rwkv_a2, rwkv_g2, rwkv_kk, rwkv_ka, rwkv_rk,
           rwkv_lnx_w, rwkv_lnx_b, hgrn_lb, hgrn_norm_w, od_w_in, od_w_out, mlstm_conv_w, mlstm_conv_b,
           mlstm_b_i, mlstm_b_f, ln_w, ln_b, moe_wg, moe_bg, moe_we, moe_be, moe_w1, moe_w3, moe_w2):
    b, s, d = x.shape
    depth = ln_w.shape[0]
    alpha = (2 * depth) ** 0.25
    lb_all = jnp.cumsum(jax.nn.softmax(hgrn_lb.astype(F32), axis=0), axis=0)
    lb_all = lb_all - lb_all[:1]
    tn = 2 * LANES
    xf = x.reshape(b * s, d)
    xh = xf
    for layer in range(depth):
        j = layer // 2
        if layer % 2 == 0:
            p = _matmul(xh, ev_w_in[j].astype(BF16), tn=tn)
            ya = _rwkv7(p, b, s, rwkv_mu[j], rwkv_w0[j], rwkv_w2[j], rwkv_a0[j], rwkv_a2[j], rwkv_g2[j],
                        rwkv_kk[j], rwkv_ka[j], rwkv_rk[j], rwkv_lnx_w[j], rwkv_lnx_b[j])
            yb = _hgrn2(p, b, s, A_COLS // LANES, lb_all[j], hgrn_norm_w[j])
            w_out = ev_w_out[j]
        else:
            w_in = od_w_in[j]
            w_in = jnp.pad(w_in, ((0, 0), (0, -w_in.shape[1] % tn))).astype(BF16)
            p = _matmul(xh, w_in, tn=tn)
            nq = C_WIDTH // LANES
            ya = _dilated_attention(p, b, s, 0, nq, 2 * nq)
            yb = _mlstm(p, b, s, 3 * nq, mlstm_conv_w[j], mlstm_conv_b[j], mlstm_b_i[j], mlstm_b_f[j])
            w_out = od_w_out[j]
        router = _router_weights(moe_wg[layer], moe_bg[layer], moe_we[layer], moe_be[layer])
        xf, slab = _proj_ln(ya, yb, w_out, xf, ln_w[layer, 0], ln_b[layer, 0], router, alpha)
        xf, xh = _moe_sublayer(xf, slab, moe_w1, moe_w3, moe_w2, layer, ln_w[layer, 1], ln_b[layer, 1], alpha)
    return xf.reshape(b, s, d)
```
